```python
import math
import jax, jax.numpy as jnp
from jax import lax
import numpy as np

D_MODEL = 1024
BATCH = 8
SEQ = 4096
DEPTH = 2

GDN_HEADS = 4
GDN_HEAD_DIM = 128
GDN_WIDTH = GDN_HEADS * GDN_HEAD_DIM
CONV_WIDTH = 4
CHUNK = 64
SWA_HEADS = 8
SWA_HEAD_DIM = 64
SWA_WIDTH = SWA_HEADS * SWA_HEAD_DIM
DILATED_CONFIGS = ((128, 1), (512, 4), (2048, 16))
BAND_BLOCK = 128
MIX_WIDTH = GDN_WIDTH + SWA_WIDTH
IN_SIZES = (3 * GDN_WIDTH, GDN_WIDTH, GDN_HEADS, GDN_HEADS, SWA_WIDTH, SWA_WIDTH, SWA_WIDTH)
IN_COLS = sum(IN_SIZES)
IN_SPLITS = tuple(int(c) for c in np.cumsum(IN_SIZES)[:-1])
D_FF = 2816
FFN_CONV_WIDTH = 3
RMS_EPS = 1e-6
L2_EPS = 1e-6

kernel_name = 'hybrid_gdn_dilated_alibi_convffn'


def rmsnorm(x, gain):
    xf = x.astype(jnp.float32)
    y = xf * lax.rsqrt(jnp.mean(xf * xf, axis=-1, keepdims=True) + RMS_EPS)
    return (y * gain.astype(jnp.float32)).astype(x.dtype)


def l2norm(x):
    xf = x.astype(jnp.float32)
    return xf * lax.rsqrt(jnp.sum(xf * xf, axis=-1, keepdims=True) + L2_EPS)


def causal_dwconv(x, w):
    k_width = w.shape[0]
    s = x.shape[1]
    xp = jnp.pad(x, ((0, 0), (k_width - 1, 0), (0, 0)))
    return sum(xp[:, j:j + s] * w[j] for j in range(k_width))


def alibi_slopes(n_heads):
    return jnp.asarray(2.0 ** (-8.0 * np.arange(1, n_heads + 1) / n_heads), dtype=jnp.float32)


def chunk_gated_delta_rule(q, k, v, g, beta):
    bn, h, s, dk = q.shape
    dv = v.shape[-1]
    n = s // CHUNK
    q = q * dk ** -0.5
    q = q.reshape(bn, h, n, CHUNK, dk)
    k = k.reshape(bn, h, n, CHUNK, dk)
    v = v.reshape(bn, h, n, CHUNK, dv)
    beta = beta.reshape(bn, h, n, CHUNK, 1)
    g = jnp.cumsum(g.reshape(bn, h, n, CHUNK), axis=-1)
    idx = jnp.arange(CHUNK)
    causal = idx[:, None] >= idx[None, :]
    strict = idx[:, None] > idx[None, :]
    decay = jnp.exp(jnp.where(causal, g[..., :, None] - g[..., None, :], -jnp.inf))
    k_beta = k * beta
    a = jnp.where(strict, jnp.einsum('bhnik,bhnjk->bhnij', k_beta, k) * decay, 0.0)
    rhs = jnp.concatenate([v * beta, k_beta * jnp.exp(g)[..., None]], axis=-1)
    sol = lax.linalg.triangular_solve(a, rhs, left_side=True, lower=True, unit_diagonal=True)
    u, w = sol[..., :dv], sol[..., dv:]
    a_qk = jnp.where(causal, jnp.einsum('bhnik,bhnjk->bhnij', q, k) * decay, 0.0)
    q_dec = q * jnp.exp(g)[..., None]
    g_last = g[..., -1:]
    k_dec = k * jnp.exp(g_last - g)[..., None]
    decay_last = jnp.exp(g_last[..., 0])

    def step(state, inp):
        qd, wc, uc, kd, aqk, dl = inp
        v_new = uc - jnp.einsum('bhik,bhkv->bhiv', wc, state)
        o = jnp.einsum('bhik,bhkv->bhiv', qd, state) + jnp.einsum('bhij,bhjv->bhiv', aqk, v_new)
        state = state * dl[..., None, None] + jnp.einsum('bhik,bhiv->bhkv', kd, v_new)
        return state, o

    xs = tuple(jnp.moveaxis(t, 2, 0) for t in (q_dec, w, u, k_dec, a_qk, decay_last))
    state0 = jnp.zeros((bn, h, dk, dv), jnp.float32)
    _, o = lax.scan(step, state0, xs)
    return jnp.moveaxis(o, 0, 2).reshape(bn, h, s, dv)


def gated_deltanet(qkv, z, b_logit, a_logit, conv_w, a_log, dt_bias, norm_gain):
    bn, s, _ = qkv.shape
    out_dtype = z.dtype
    qkv = jax.nn.silu(causal_dwconv(qkv, conv_w))
    q, k, v = jnp.split(qkv, 3, axis=-1)
    heads = lambda t: t.reshape(bn, s, GDN_HEADS, GDN_HEAD_DIM).transpose(0, 2, 1, 3)
    q = l2norm(heads(q))
    k = l2norm(heads(k))
    v = heads(v).astype(jnp.float32)
    beta = jax.nn.sigmoid(b_logit.astype(jnp.float32)).transpose(0, 2, 1)
    g = (-jnp.exp(a_log.astype(jnp.float32))
         * jax.nn.softplus(a_logit.astype(jnp.float32) + dt_bias.astype(jnp.float32))).transpose(0, 2, 1)
    o = chunk_gated_delta_rule(q, k, v, g, beta).transpose(0, 2, 1, 3)
    zf = z.reshape(bn, s, GDN_HEADS, GDN_HEAD_DIM).astype(jnp.float32)
    y = (o * lax.rsqrt(jnp.mean(o * o, axis=-1, keepdims=True) + RMS_EPS)
         * norm_gain.astype(jnp.float32) * jax.nn.silu(zf))
    return y.reshape(bn, s, GDN_WIDTH).astype(out_dtype)


def banded_causal_attention(q, k, v, n_back, step, slopes):
    nb, h, l, dh = q.shape
    nblk = -(-l // BAND_BLOCK)
    lp = nblk * BAND_BLOCK
    qb = jnp.pad(q, ((0, 0), (0, 0), (0, lp - l), (0, 0))).reshape(nb, h, nblk, BAND_BLOCK, dh)

    def band(t):
        tp = jnp.pad(t, ((0, 0), (0, 0), (BAND_BLOCK, lp - l), (0, 0)))
        prev = tp[:, :, :lp].reshape(nb, h, nblk, BAND_BLOCK, dh)
        cur = tp[:, :, BAND_BLOCK:].reshape(nb, h, nblk, BAND_BLOCK, dh)
        return jnp.concatenate([prev, cur], axis=3)

    kb, vb = band(k), band(v)
    s = jnp.einsum('nhbqd,nhbkd->nhbqk', qb, kb).astype(jnp.float32) * dh ** -0.5
    qi = jnp.arange(BAND_BLOCK)[:, None]
    kj = jnp.arange(2 * BAND_BLOCK)[None, :]
    dist = qi + BAND_BLOCK - kj
    key_pos = (jnp.arange(nblk) * BAND_BLOCK - BAND_BLOCK)[:, None, None] + kj
    valid = (dist >= 0) & (dist <= n_back) & (key_pos >= 0)
    alibi = -slopes[:, None, None, None] * (dist * step).astype(jnp.float32)
    s = jnp.where(valid, s + alibi, -jnp.inf)
    m = jnp.max(s, axis=-1, keepdims=True)
    p = jnp.exp(s - m)
    den = jnp.sum(p, axis=-1, keepdims=True)
    o = jnp.einsum('nhbqk,nhbkd->nhbqd', p, vb.astype(jnp.float32)) / den
    lse = (m + jnp.log(den))[..., 0]
    return o.reshape(nb, h, lp, dh)[:, :, :l], lse.reshape(nb, h, lp)[:, :, :l]


def dilated_attention(q, k, v, slopes):
    bn, s, h, dh = q.shape
    outs, lses = [], []
    for window, dilation in DILATED_CONFIGS:
        l = s // dilation
        sub = lambda t: t.reshape(bn, l, dilation, h, dh).transpose(0, 2, 3, 1, 4).reshape(bn * dilation, h, l, dh)
        o, lse = banded_causal_attention(sub(q), sub(k), sub(v), window // dilation, dilation, slopes)
        outs.append(o.reshape(bn, dilation, h, l, dh).transpose(0, 3, 1, 2, 4).reshape(bn, s, h, dh))
        lses.append(lse.reshape(bn, dilation, h, l).transpose(0, 3, 1, 2).reshape(bn, s, h))
    weights = jax.nn.softmax(jnp.stack(lses), axis=0)
    return jnp.einsum('gbsh,gbshd->bshd', weights, jnp.stack(outs))


def setup_inputs(seed: int = 0) -> dict:
    key = jax.random.key(seed)
    ks = jax.random.split(key, 16)
    nrm = lambda kk, shape, fan_in: jax.random.normal(kk, shape, jnp.float32) * fan_in ** -0.5
    x = jax.random.normal(ks[0], (BATCH, SEQ, D_MODEL), jnp.float32)
    ln1 = 1.0 + 0.02 * jax.random.normal(ks[1], (DEPTH, D_MODEL), jnp.float32)
    w_in = nrm(ks[2], (DEPTH, D_MODEL, IN_COLS), D_MODEL)
    conv_qkv = nrm(ks[3], (DEPTH, CONV_WIDTH, 3 * GDN_WIDTH), CONV_WIDTH)
    a_log = jnp.log(jax.random.uniform(ks[4], (DEPTH, GDN_HEADS), jnp.float32, minval=1.0, maxval=16.0))
    dt = jnp.exp(jax.random.uniform(ks[5], (DEPTH, GDN_HEADS), jnp.float32,
                                    minval=math.log(1e-3), maxval=math.log(1e-1)))
    dt_bias = jnp.log(jnp.expm1(dt))
    gdn_norm = 1.0 + 0.02 * jax.random.normal(ks[6], (DEPTH, GDN_HEAD_DIM), jnp.float32)
    w_out = nrm(ks[7], (DEPTH, MIX_WIDTH, D_MODEL), MIX_WIDTH)
    ln2 = 1.0 + 0.02 * jax.random.normal(ks[8], (DEPTH, D_MODEL), jnp.float32)
    w_gate = nrm(ks[9], (DEPTH, D_MODEL, D_FF), D_MODEL)
    w_up = nrm(ks[10], (DEPTH, D_MODEL, D_FF), D_MODEL)
    ffn_conv = nrm(ks[11], (DEPTH, FFN_CONV_WIDTH, D_FF), FFN_CONV_WIDTH)
    w_down = nrm(ks[12], (DEPTH, D_FF, D_MODEL), D_FF)
    ln_f = 1.0 + 0.02 * jax.random.normal(ks[13], (D_MODEL,), jnp.float32)
    return {'x': x, 'ln1': ln1, 'w_in': w_in, 'conv_qkv': conv_qkv, 'a_log': a_log, 'dt_bias': dt_bias,
            'gdn_norm': gdn_norm, 'w_out': w_out, 'ln2': ln2, 'w_gate': w_gate, 'w_up': w_up,
            'ffn_conv': ffn_conv, 'w_down': w_down, 'ln_f': ln_f}


def reference(x, ln1, w_in, conv_qkv, a_log, dt_bias, gdn_norm, w_out, ln2, w_gate, w_up, ffn_conv, w_down, ln_f):
    bn, s, _ = x.shape
    slopes = alibi_slopes(SWA_HEADS)
    heads_b = lambda t: t.reshape(bn, s, SWA_HEADS, SWA_HEAD_DIM)
    for l in range(DEPTH):
        h = rmsnorm(x, ln1[l])
        proj = h @ w_in[l]
        qkv_a, z_a, b_a, a_a, q_b, k_b, v_b = jnp.split(proj, IN_SPLITS, axis=-1)
        y_a = gated_deltanet(qkv_a, z_a, b_a, a_a, conv_qkv[l], a_log[l], dt_bias[l], gdn_norm[l])
        y_b = dilated_attention(heads_b(q_b), heads_b(k_b), heads_b(v_b), slopes)
        y_b = y_b.reshape(bn, s, SWA_WIDTH).astype(x.dtype)
        x = x + jnp.concatenate([y_a, y_b], axis=-1) @ w_out[l]
        h = rmsnorm(x, ln2[l])
        gate = causal_dwconv(h @ w_gate[l], ffn_conv[l])
        x = x + (jax.nn.silu(gate) * (h @ w_up[l])) @ w_down[l]
    return rmsnorm(x, ln_f)
```

```python
import functools

import numpy as np
import jax
import jax.numpy as jnp
from jax import lax
from jax.experimental import pallas as pl
from jax.experimental.pallas import tpu as pltpu

F32 = jnp.float32
BF16 = jnp.bfloat16

D_MODEL = 1024
GDN_HEADS = 4
GDN_HEAD_DIM = 128
GDN_WIDTH = GDN_HEADS * GDN_HEAD_DIM
CONV_WIDTH = 4
CHUNK = 64
SWA_HEADS = 8
SWA_HEAD_DIM = 64
SWA_WIDTH = SWA_HEADS * SWA_HEAD_DIM
DILATED_CONFIGS = ((128, 1), (512, 4), (2048, 16))
BAND_BLOCK = 128
D_FF = 2816
FFN_CONV_WIDTH = 3
RMS_EPS = 1e-6
L2_EPS = 1e-6
MASK_VALUE = -1e30

LANES = 128
SUBLANES = 8
VMEM_LIMIT_BYTES = 56 * 1024 * 1024

TOKEN_TILE = 512
FFN_TOKEN_TILE = 256
GDN_TOKEN_TILE = 512
ATTN_MAX_QUERY_ROWS = 512


def _sigmoid(v):
    return 1.0 / (1.0 + jnp.exp(-v))


def _dot(a, b):
    return jnp.dot(a, b, preferred_element_type=F32)


def _dot_nt(a, b):
    return lax.dot_general(a, b, (((1,), (1,)), ((), ())), preferred_element_type=F32)


def _rmsnorm(v, gain):
    return v * lax.rsqrt(jnp.mean(v * v, axis=-1, keepdims=True) + RMS_EPS) * gain


def _causal_conv_from_ext(ext_ref, cw_ref, cur, first_step, tm, width):
    @pl.when(first_step)
    def _():
        ext_ref[0:SUBLANES, :] = jnp.zeros((SUBLANES, ext_ref.shape[1]), F32)

    @pl.when(jnp.logical_not(first_step))
    def _():
        ext_ref[0:SUBLANES, :] = ext_ref[tm:tm + SUBLANES, :]

    ext_ref[SUBLANES:SUBLANES + tm, :] = cur
    acc = cur * cw_ref[width - 1:width, :]
    for s in range(1, width):
        acc = acc + ext_ref[pl.ds(SUBLANES - s, tm), :] * cw_ref[width - 1 - s:width - s, :]
    return acc


def _inproj_kernel(x_ref, ln_ref, wa_ref, wg_ref, wb_ref, cw_ref, alog_ref, dtb_ref,
                   qkv_ref, z_ref, g_ref, beta_ref, qkvb_ref, ext_ref, *, tm):
    first_step = pl.program_id(1) == 0
    hb = _rmsnorm(x_ref[0], ln_ref[...]).astype(BF16)

    qkvb_ref[0] = _dot(hb, wb_ref[...]).astype(BF16)

    yg = _dot(hb, wg_ref[...])
    a_l = yg[:, :GDN_WIDTH] + dtb_ref[...]
    softplus = jnp.maximum(a_l, 0.0) + jnp.log1p(jnp.exp(-jnp.abs(a_l)))
    g_ref[0] = -jnp.exp(alog_ref[...]) * softplus
    beta_ref[0] = _sigmoid(yg[:, GDN_WIDTH:])

    ya = _dot(hb, wa_ref[...])
    z_ref[0] = ya[:, 3 * GDN_WIDTH:].astype(BF16)

    conv = _causal_conv_from_ext(ext_ref, cw_ref, ya[:, :3 * GDN_WIDTH], first_step, tm, CONV_WIDTH)
    act = conv * _sigmoid(conv)
    for slot in range(2 * GDN_HEADS):
        lo = slot * GDN_HEAD_DIM
        blk = act[:, lo:lo + GDN_HEAD_DIM]
        nrm = blk * lax.rsqrt(jnp.sum(blk * blk, axis=-1, keepdims=True) + L2_EPS)
        qkv_ref[0, :, lo:lo + GDN_HEAD_DIM] = nrm.astype(BF16)
    qkv_ref[0, :, 2 * GDN_WIDTH:] = act[:, 2 * GDN_WIDTH:].astype(BF16)


def _in_proj(x, ln, wa, wg, wb, cw, alog_x, dtb_x):
    bn, s, d = x.shape
    tm = min(TOKEN_TILE, s)
    grid = (bn, s // tm)
    tok = lambda w: pl.BlockSpec((1, tm, w), lambda b, j: (b, j, 0))
    const = lambda a: pl.BlockSpec(a.shape, lambda b, j: (0,) * a.ndim)
    out_shape = (
        jax.ShapeDtypeStruct((bn, s, 3 * GDN_WIDTH), BF16),
        jax.ShapeDtypeStruct((bn, s, GDN_WIDTH), BF16),
        jax.ShapeDtypeStruct((bn, s, GDN_WIDTH), F32),
        jax.ShapeDtypeStruct((bn, s, GDN_WIDTH), F32),
        jax.ShapeDtypeStruct((bn, s, 3 * SWA_WIDTH), BF16),
    )
    return pl.pallas_call(
        functools.partial(_inproj_kernel, tm=tm),
        grid=grid,
        in_specs=[tok(d), const(ln), const(wa), const(wg), const(wb), const(cw),
                  const(alog_x), const(dtb_x)],
        out_specs=(tok(3 * GDN_WIDTH), tok(GDN_WIDTH), tok(GDN_WIDTH), tok(GDN_WIDTH),
                   tok(3 * SWA_WIDTH)),
        out_shape=out_shape,
        scratch_shapes=[pltpu.VMEM((tm + SUBLANES, 3 * GDN_WIDTH), F32)],
        compiler_params=pltpu.CompilerParams(
            dimension_semantics=("arbitrary", "arbitrary"),
            vmem_limit_bytes=VMEM_LIMIT_BYTES),
        name="in_proj",
    )(x, ln, wa, wg, wb, cw, alog_x, dtb_x)


def _cumsum_rows(v):
    n = v.shape[0]
    row = lax.broadcasted_iota(jnp.int32, v.shape, 0)
    s = 1
    while s < n:
        v = v + jnp.where(row >= s, pltpu.roll(v, s, axis=0), 0.0)
        s *= 2
    return v


def _unit_lower_inverse_minus_identity(a):
    n = a.shape[0]
    row = lax.broadcasted_iota(jnp.int32, a.shape, 0)
    col = lax.broadcasted_iota(jnp.int32, a.shape, 1)
    nmat = -jnp.where((row // 2 == col // 2) & (row > col), a, 0.0)
    m = 2
    while m < n:
        sel = (row // (2 * m) == col // (2 * m)) & (row % (2 * m) >= m) & (col % (2 * m) < m)
        am = jnp.where(sel, a, 0.0)
        nb = nmat.astype(BF16)
        x = am + _dot(am.astype(BF16), nb)
        y = x + _dot(nb, x.astype(BF16))
        nmat = nmat - y
        m *= 2
    return nmat


def _gdn_kernel(q_ref, k_ref, v_ref, g_ref, beta_ref, z_ref, gn_ref, y_ref,
                state_ref, u_ref, w_ref, qd_ref, aqk_ref, kdt_ref, dl_ref, *, n_chunks):
    @pl.when(pl.program_id(1) == 0)
    def _():
        state_ref[...] = jnp.zeros(state_ref.shape, F32)

    scale = GDN_HEAD_DIM ** -0.5
    row = lax.broadcasted_iota(jnp.int32, (CHUNK, CHUNK), 0)
    col = lax.broadcasted_iota(jnp.int32, (CHUNK, CHUNK), 1)
    causal = row >= col
    strict = row > col

    def prepare(c, carry):
        r0 = pl.multiple_of(c * CHUNK, CHUNK)
        rows = pl.ds(r0, CHUNK)
        for h in range(GDN_HEADS):
            lanes = slice(h * GDN_HEAD_DIM, (h + 1) * GDN_HEAD_DIM)
            q = q_ref[0, rows, lanes]
            k = k_ref[0, rows, lanes]
            v = v_ref[0, rows, lanes].astype(F32)
            beta = beta_ref[0, rows, lanes]
            gc = _cumsum_rows(g_ref[0, rows, lanes])
            g_last = gc[CHUNK - 1:CHUNK, :]
            diff = gc[:, :CHUNK] - gc.T[:CHUNK, :]
            decay = jnp.exp(jnp.where(causal, diff, MASK_VALUE))
            kq = _dot_nt(jnp.concatenate([k, q], axis=0), k)
            a = jnp.where(strict, kq[:CHUNK] * decay, 0.0) * beta[:, :CHUNK]
            nmat = _unit_lower_inverse_minus_identity(a).astype(BF16)
            e_gc = jnp.exp(gc)
            kf = k.astype(F32)
            rhs = jnp.concatenate([v * beta, kf * (beta * e_gc)], axis=1)
            sol = rhs + _dot(nmat, rhs.astype(BF16))
            u_ref[h, rows, :] = sol[:, :GDN_HEAD_DIM]
            w_ref[h, rows, :] = sol[:, GDN_HEAD_DIM:].astype(BF16)
            qd_ref[h, rows, :] = (q.astype(F32) * (scale * e_gc)).astype(BF16)
            aqk_ref[h, rows, :] = (kq[CHUNK:] * decay * scale).astype(BF16)
            k_dec = kf * jnp.exp(g_last - gc)
            kdt_ref[h, pl.ds(pl.multiple_of(c * GDN_HEAD_DIM, GDN_HEAD_DIM), GDN_HEAD_DIM), :] = (
                k_dec.T.astype(BF16))
            dl_ref[h, pl.ds(pl.multiple_of(c * SUBLANES, SUBLANES), SUBLANES), :] = (
                jnp.broadcast_to(jnp.exp(g_last), (SUBLANES, GDN_HEAD_DIM)))
        return carry

    lax.fori_loop(0, n_chunks, prepare, 0)

    def recur(c, carry):
        r0 = pl.multiple_of(c * CHUNK, CHUNK)
        rows = pl.ds(r0, CHUNK)
        for h in range(GDN_HEADS):
            lanes = slice(h * GDN_HEAD_DIM, (h + 1) * GDN_HEAD_DIM)
            state = state_ref[h]
            sb = state.astype(BF16)
            wq = jnp.concatenate([w_ref[h, rows, :], qd_ref[h, rows, :]], axis=0)
            r = _dot(wq, sb)
            v_new = u_ref[h, rows, :] - r[:CHUNK]
            vb = v_new.astype(BF16)
            o = r[CHUNK:] + _dot(aqk_ref[h, rows, :], vb)
            kdt = kdt_ref[h, pl.ds(pl.multiple_of(c * GDN_HEAD_DIM, GDN_HEAD_DIM), GDN_HEAD_DIM), :]
            dl = dl_ref[h, pl.ds(pl.multiple_of(c * SUBLANES, SUBLANES), SUBLANES), :]
            state_ref[h] = state * dl[0:1, :] + _dot(kdt, vb)
            zf = z_ref[0, rows, lanes].astype(F32)
            y = _rmsnorm(o, gn_ref[...]) * (zf * _sigmoid(zf))
            y_ref[0, rows, lanes] = y.astype(BF16)
        return carry

    lax.fori_loop(0, n_chunks, recur, 0)


def _gdn(qkv, z, g, beta, gn):
    bn, s, _ = qkv.shape
    tc = min(GDN_TOKEN_TILE, s)
    n_chunks = tc // CHUNK
    grid = (bn, s // tc)
    col = lambda i: pl.BlockSpec((1, tc, GDN_WIDTH), lambda b, j: (b, j, i))
    return pl.pallas_call(
        functools.partial(_gdn_kernel, n_chunks=n_chunks),
        grid=grid,
        in_specs=[col(0), col(1), col(2), col(0), col(0), col(0),
                  pl.BlockSpec(gn.shape, lambda b, j: (0, 0))],
        out_specs=col(0),
        out_shape=jax.ShapeDtypeStruct((bn, s, GDN_WIDTH), BF16),
        scratch_shapes=[
            pltpu.VMEM((GDN_HEADS, GDN_HEAD_DIM, GDN_HEAD_DIM), F32),
            pltpu.VMEM((GDN_HEADS, tc, GDN_HEAD_DIM), F32),
            pltpu.VMEM((GDN_HEADS, tc, GDN_HEAD_DIM), BF16),
            pltpu.VMEM((GDN_HEADS, tc, GDN_HEAD_DIM), BF16),
            pltpu.VMEM((GDN_HEADS, tc, CHUNK), BF16),
            pltpu.VMEM((GDN_HEADS, n_chunks * GDN_HEAD_DIM, CHUNK), BF16),
            pltpu.VMEM((GDN_HEADS, n_chunks * SUBLANES, GDN_HEAD_DIM), F32),
        ],
        compiler_params=pltpu.CompilerParams(
            dimension_semantics=("arbitrary", "arbitrary"),
            vmem_limit_bytes=VMEM_LIMIT_BYTES),
        name="gdn",
    )(qkv, qkv, qkv, g, beta, z, gn)


def _attn_kernel(q_ref, kc_ref, kp_ref, vc_ref, vp_ref, bias_ref, o_ref, lse_ref, *, n_sub):
    first_step = pl.program_id(2) == 0
    scale = SWA_HEAD_DIM ** -0.5
    pair_w = 2 * SWA_HEAD_DIM
    lane = lax.broadcasted_iota(jnp.int32, (1, pair_w), 1)
    low = lane < SWA_HEAD_DIM
    lane_full = lax.broadcasted_iota(jnp.int32, (BAND_BLOCK, LANES), 1)
    zero = jnp.zeros((), BF16)

    for sblk in range(n_sub):
        rows = slice(sblk * BAND_BLOCK, (sblk + 1) * BAND_BLOCK)
        prev_rows = slice((sblk - 1) * BAND_BLOCK, sblk * BAND_BLOCK)
        variant = jnp.where(first_step, 1, 0) if sblk == 0 else 0
        lse_tile = jnp.zeros((BAND_BLOCK, LANES), F32)
        for p in range(SWA_HEADS // 2):
            lanes = slice(p * pair_w, (p + 1) * pair_w)
            q = q_ref[0, rows, lanes]
            if sblk == 0:
                k_prev, v_prev = kp_ref[0, :, lanes], vp_ref[0, :, lanes]
            else:
                k_prev, v_prev = kc_ref[0, prev_rows, lanes], vc_ref[0, prev_rows, lanes]
            kcat = jnp.concatenate([k_prev, kc_ref[0, rows, lanes]], axis=0)
            vcat = jnp.concatenate([v_prev, vc_ref[0, rows, lanes]], axis=0)
            outs = []
            for hh in range(2):
                h = 2 * p + hh
                km = jnp.where(low if hh == 0 else jnp.logical_not(low), kcat, zero)
                sc = _dot_nt(q, km) * scale + bias_ref[variant, h]
                m = jnp.max(sc, axis=-1, keepdims=True)
                pr = jnp.exp(sc - m)
                den = jnp.sum(pr, axis=-1, keepdims=True)
                outs.append(_dot(pr.astype(BF16), vcat) / den)
                lse_tile = lse_tile + jnp.where(lane_full == h, m + jnp.log(den), 0.0)
            o_ref[0, rows, lanes] = jnp.where(low, outs[0], outs[1]).astype(BF16)
        lse_ref[0, rows, :] = lse_tile


def _attn_bias(n_back, step):
    qi = np.arange(BAND_BLOCK)[:, None]
    kj = np.arange(2 * BAND_BLOCK)[None, :]
    dist = qi + BAND_BLOCK - kj
    valid = (dist >= 0) & (dist <= n_back)
    slopes = 2.0 ** (-8.0 * np.arange(1, SWA_HEADS + 1) / SWA_HEADS)
    alibi = -slopes[:, None, None] * (dist * step).astype(np.float64)[None]
    regular = np.where(valid[None], alibi, MASK_VALUE)
    first = np.where((valid & (kj >= BAND_BLOCK))[None], alibi, MASK_VALUE)
    return jnp.asarray(np.stack([regular, first]), dtype=F32)


def _attn(qkvb, window, dilation):
    bn, s, width = qkvb.shape
    l = s // dilation
    view = qkvb.reshape(bn, l, dilation * width)
    qb = min(ATTN_MAX_QUERY_ROWS, l)
    n_sub = qb // BAND_BLOCK
    grid = (bn, dilation, l // qb)
    bias = _attn_bias(window // dilation, dilation)
    cur = lambda part: pl.BlockSpec((1, qb, SWA_WIDTH), lambda b, r, i: (b, i, 3 * r + part))
    prev = lambda part: pl.BlockSpec(
        (1, BAND_BLOCK, SWA_WIDTH), lambda b, r, i: (b, jnp.maximum(i * n_sub - 1, 0), 3 * r + part))
    o, lse = pl.pallas_call(
        functools.partial(_attn_kernel, n_sub=n_sub),
        grid=grid,
        in_specs=[cur(0), cur(1), prev(1), cur(2), prev(2),
                  pl.BlockSpec(bias.shape, lambda b, r, i: (0, 0, 0, 0))],
        out_specs=(pl.BlockSpec((1, qb, SWA_WIDTH), lambda b, r, i: (b, i, r)),
                   pl.BlockSpec((1, qb, LANES), lambda b, r, i: (b, i, r))),
        out_shape=(jax.ShapeDtypeStruct((bn, l, dilation * SWA_WIDTH), BF16),
                   jax.ShapeDtypeStruct((bn, l, dilation * LANES), F32)),
        compiler_params=pltpu.CompilerParams(
            dimension_semantics=("arbitrary", "arbitrary", "arbitrary"),
            vmem_limit_bytes=VMEM_LIMIT_BYTES),
        name=f"attn_d{dilation}",
    )(view, view, view, view, view, bias)
    return o.reshape(bn, s, SWA_WIDTH), lse.reshape(bn, s, LANES)


def _outproj_kernel(x_ref, ya_ref, o1_ref, o2_ref, o3_ref, l1_ref, l2_ref, l3_ref,
                    expand_ref, w_ref, out_ref):
    lses = (l1_ref[0], l2_ref[0], l3_ref[0])
    m = jnp.maximum(jnp.maximum(lses[0], lses[1]), lses[2])
    es = [jnp.exp(v - m) for v in lses]
    inv = 1.0 / (es[0] + es[1] + es[2])
    yb = None
    for e, o_ref in zip(es, (o1_ref, o2_ref, o3_ref)):
        wgt = e * inv
        hi = wgt.astype(BF16)
        lo = (wgt - hi.astype(F32)).astype(BF16)
        wide = _dot(hi, expand_ref[...]) + _dot(lo, expand_ref[...])
        term = wide * o_ref[0].astype(F32)
        yb = term if yb is None else yb + term
    y = jnp.concatenate([ya_ref[0], yb.astype(BF16)], axis=-1)
    out_ref[0] = x_ref[0] + _dot(y, w_ref[...])


def _out_proj(x, ya, os_, lses, w_out):
    bn, s, d = x.shape
    tm = min(TOKEN_TILE, s)
    expand = jnp.asarray(
        (np.arange(LANES)[:, None] == (np.arange(SWA_WIDTH)[None, :] // SWA_HEAD_DIM)), dtype=BF16)
    tok = lambda w: pl.BlockSpec((1, tm, w), lambda b, j: (b, j, 0))
    const = lambda a: pl.BlockSpec(a.shape, lambda b, j: (0,) * a.ndim)
    return pl.pallas_call(
        _outproj_kernel,
        grid=(bn, s // tm),
        in_specs=[tok(d), tok(GDN_WIDTH)] + [tok(SWA_WIDTH)] * 3 + [tok(LANES)] * 3
                 + [const(expand), const(w_out)],
        out_specs=tok(d),
        out_shape=jax.ShapeDtypeStruct((bn, s, d), F32),
        compiler_params=pltpu.CompilerParams(
            dimension_semantics=("arbitrary", "arbitrary"),
            vmem_limit_bytes=VMEM_LIMIT_BYTES),
        name="out_proj",
    )(x, ya, *os_, *lses, expand, w_out)


def _ffn_kernel(x_ref, ln_ref, wg_ref, wu_ref, cw_ref, wd_ref, lnf_ref, out_ref, ext_ref,
                *, tm, final_norm):
    first_step = pl.program_id(1) == 0
    x = x_ref[0]
    hb = _rmsnorm(x, ln_ref[...]).astype(BF16)
    gate = _causal_conv_from_ext(ext_ref, cw_ref, _dot(hb, wg_ref[...]), first_step, tm,
                                 FFN_CONV_WIDTH)
    act = (gate * _sigmoid(gate) * _dot(hb, wu_ref[...])).astype(BF16)
    y = x + _dot(act, wd_ref[...])
    if final_norm:
        y = _rmsnorm(y, lnf_ref[...])
    out_ref[0] = y


def _ffn(x, ln, wg, wu, cw, wd, lnf, final_norm):
    bn, s, d = x.shape
    tm = min(FFN_TOKEN_TILE, s)
    tok = pl.BlockSpec((1, tm, d), lambda b, j: (b, j, 0))
    const = lambda a: pl.BlockSpec(a.shape, lambda b, j: (0,) * a.ndim,
                                   pipeline_mode=pl.Buffered(1))
    return pl.pallas_call(
        functools.partial(_ffn_kernel, tm=tm, final_norm=final_norm),
        grid=(bn, s // tm),
        in_specs=[tok, const(ln), const(wg), const(wu), const(cw), const(wd), const(lnf)],
        out_specs=tok,
        out_shape=jax.ShapeDtypeStruct((bn, s, d), F32),
        scratch_shapes=[pltpu.VMEM((tm + SUBLANES, D_FF), F32)],
        compiler_params=pltpu.CompilerParams(
            dimension_semantics=("arbitrary", "arbitrary"),
            vmem_limit_bytes=VMEM_LIMIT_BYTES),
        name="ffn",
    )(x, ln, wg, wu, cw, wd, lnf)


def kernel(x, ln1, w_in, conv_qkv, a_log, dt_bias, gdn_norm, w_out, ln2, w_gate, w_up,
           ffn_conv, w_down, ln_f):
    bn, s, d = x.shape
    depth = w_in.shape[0]
    assert d == D_MODEL and s % GDN_TOKEN_TILE == 0 and s % TOKEN_TILE == 0
    for window, dilation in DILATED_CONFIGS:
        assert window // dilation <= BAND_BLOCK and (s // dilation) % BAND_BLOCK == 0

    c_z = 3 * GDN_WIDTH
    c_b = c_z + GDN_WIDTH
    c_a = c_b + GDN_HEADS
    c_q = c_a + GDN_HEADS
    rep = lambda v: jnp.repeat(v, GDN_HEAD_DIM, axis=-1)
    row = lambda v: v.reshape(1, -1)

    for l in range(depth):
        w = w_in[l]
        wa = w[:, :c_b].astype(BF16)
        wg = jnp.concatenate([rep(w[:, c_a:c_q]), rep(w[:, c_b:c_a])], axis=1).astype(BF16)
        wb = w[:, c_q:].astype(BF16)
        qkv, z, g, beta, qkvb = _in_proj(
            x, row(ln1[l]), wa, wg, wb, conv_qkv[l], row(rep(a_log[l])), row(rep(dt_bias[l])))
        ya = _gdn(qkv, z, g, beta, row(gdn_norm[l]))
        os_, lses = zip(*[_attn(qkvb, window, dilation) for window, dilation in DILATED_CONFIGS])
        x = _out_proj(x, ya, os_, lses, w_out[l].astype(BF16))
        x = _ffn(x, row(ln2[l]), w_gate[l].astype(BF16), w_up[l].astype(BF16), ffn_conv[l],
                 w_down[l].astype(BF16), row(ln_f), final_norm=(l == depth - 1))
    return x
```

```python
import functools

import numpy as np
import jax
import jax.numpy as jnp
from jax import lax
from jax.experimental import pallas as pl
from jax.experimental.pallas import tpu as pltpu

F32 = jnp.float32
BF16 = jnp.bfloat16

D_MODEL = 1024
GDN_HEADS = 4
GDN_HEAD_DIM = 128
GDN_WIDTH = GDN_HEADS * GDN_HEAD_DIM
CONV_WIDTH = 4
CHUNK = 64
SWA_HEADS = 8
SWA_HEAD_DIM = 64
SWA_WIDTH = SWA_HEADS * SWA_HEAD_DIM
DILATED_CONFIGS = ((128, 1), (512, 4), (2048, 16))
BAND_BLOCK = 128
D_FF = 2816
FFN_CONV_WIDTH = 3
RMS_EPS = 1e-6
L2_EPS = 1e-6
MASK_VALUE = -1e30

LANES = 128
SUBLANES = 8
VMEM_LIMIT_BYTES = 56 * 1024 * 1024

TOKEN_TILE = 512
FFN_TOKEN_TILE = 256
GDN_TOKEN_TILE = 512
ATTN_MAX_QUERY_ROWS = 512


def _sigmoid(v):
    return 1.0 / (1.0 + jnp.exp(-v))


def _dot(a, b):
    return jnp.dot(a, b, preferred_element_type=F32)


def _dot_nt(a, b):
    return lax.dot_general(a, b, (((1,), (1,)), ((), ())), preferred_element_type=F32)


def _rmsnorm(v, gain):
    return v * lax.rsqrt(jnp.mean(v * v, axis=-1, keepdims=True) + RMS_EPS) * gain


def _causal_conv_from_ext(ext_ref, cw_ref, cur, first_step, tm, width):
    @pl.when(first_step)
    def _():
        ext_ref[0:SUBLANES, :] = jnp.zeros((SUBLANES, ext_ref.shape[1]), F32)

    @pl.when(jnp.logical_not(first_step))
    def _():
        ext_ref[0:SUBLANES, :] = ext_ref[tm:tm + SUBLANES, :]

    ext_ref[SUBLANES:SUBLANES + tm, :] = cur
    acc = cur * cw_ref[width - 1:width, :]
    for s in range(1, width):
        acc = acc + ext_ref[pl.ds(SUBLANES - s, tm), :] * cw_ref[width - 1 - s:width - s, :]
    return acc


def _inproj_kernel(x_ref, ln_ref, wa_ref, wg_ref, wb_ref, cw_ref, alog_ref, dtb_ref,
                   qkv_ref, z_ref, g_ref, beta_ref, *rest, tm):
    qkvb_refs, (ext_ref, stage_ref) = rest[:-2], rest[-2:]
    first_step = pl.program_id(1) == 0
    hb = _rmsnorm(x_ref[0], ln_ref[...]).astype(BF16)

    width = 3 * SWA_WIDTH
    yb = _dot(hb, wb_ref[...])
    for cb in range(width // LANES):
        stage_ref[cb] = yb[:, cb * LANES:(cb + 1) * LANES]
    for (_, dilation), out_ref in zip(DILATED_CONFIGS, qkvb_refs):
        if dilation == 1:
            out_ref[0] = yb.astype(BF16)
            continue
        for r in range(dilation):
            rows = pl.ds(r, tm // dilation, stride=dilation)
            for cb in range(width // LANES):
                lo = r * width + cb * LANES
                out_ref[0, :, lo:lo + LANES] = stage_ref[cb, rows, :].astype(BF16)

    yg = _dot(hb, wg_ref[...])
    a_l = yg[:, :GDN_WIDTH] + dtb_ref[...]
    softplus = jnp.maximum(a_l, 0.0) + jnp.log1p(jnp.exp(-jnp.abs(a_l)))
    g_ref[0] = -jnp.exp(alog_ref[...]) * softplus
    beta_ref[0] = _sigmoid(yg[:, GDN_WIDTH:])

    ya = _dot(hb, wa_ref[...])
    z_ref[0] = ya[:, 3 * GDN_WIDTH:].astype(BF16)

    conv = _causal_conv_from_ext(ext_ref, cw_ref, ya[:, :3 * GDN_WIDTH], first_step, tm, CONV_WIDTH)
    act = conv * _sigmoid(conv)
    for slot in range(2 * GDN_HEADS):
        lo = slot * GDN_HEAD_DIM
        blk = act[:, lo:lo + GDN_HEAD_DIM]
        nrm = blk * lax.rsqrt(jnp.sum(blk * blk, axis=-1, keepdims=True) + L2_EPS)
        qkv_ref[0, :, lo:lo + GDN_HEAD_DIM] = nrm.astype(BF16)
    qkv_ref[0, :, 2 * GDN_WIDTH:] = act[:, 2 * GDN_WIDTH:].astype(BF16)


def _in_proj(x, ln, wa, wg, wb, cw, alog_x, dtb_x):
    bn, s, d = x.shape
    tm = min(TOKEN_TILE, s)
    grid = (bn, s // tm)
    tok = lambda w: pl.BlockSpec((1, tm, w), lambda b, j: (b, j, 0))
    const = lambda a: pl.BlockSpec(a.shape, lambda b, j: (0,) * a.ndim)
    dils = [dil for _, dil in DILATED_CONFIGS]
    out_shape = (
        jax.ShapeDtypeStruct((bn, s, 3 * GDN_WIDTH), BF16),
        jax.ShapeDtypeStruct((bn, s, GDN_WIDTH), BF16),
        jax.ShapeDtypeStruct((bn, s, GDN_WIDTH), F32),
        jax.ShapeDtypeStruct((bn, s, GDN_WIDTH), F32),
    ) + tuple(jax.ShapeDtypeStruct((bn, s // dil, dil * 3 * SWA_WIDTH), BF16) for dil in dils)
    return pl.pallas_call(
        functools.partial(_inproj_kernel, tm=tm),
        grid=grid,
        in_specs=[tok(d), const(ln), const(wa), const(wg), const(wb), const(cw),
                  const(alog_x), const(dtb_x)],
        out_specs=(tok(3 * GDN_WIDTH), tok(GDN_WIDTH), tok(GDN_WIDTH), tok(GDN_WIDTH))
                  + tuple(pl.BlockSpec((1, tm // dil, dil * 3 * SWA_WIDTH), lambda b, j: (b, j, 0))
                          for dil in dils),
        out_shape=out_shape,
        scratch_shapes=[pltpu.VMEM((tm + SUBLANES, 3 * GDN_WIDTH), F32),
                        pltpu.VMEM((3 * SWA_WIDTH // LANES, tm, LANES), F32)],
        compiler_params=pltpu.CompilerParams(
            dimension_semantics=("arbitrary", "arbitrary"),
            vmem_limit_bytes=VMEM_LIMIT_BYTES),
        name="in_proj",
    )(x, ln, wa, wg, wb, cw, alog_x, dtb_x)


def _bmm(a, b):
    return lax.dot_general(a, b, (((2,), (1,)), ((0,), (0,))), preferred_element_type=F32)


def _bmm_nt(a, b):
    return lax.dot_general(a, b, (((2,), (2,)), ((0,), (0,))), preferred_element_type=F32)


def _chunk_cumsum(v):
    row = lax.broadcasted_iota(jnp.int32, v.shape, 0) % CHUNK
    s = 1
    while s < CHUNK:
        v = v + jnp.where(row >= s, pltpu.roll(v, s, axis=0), 0.0)
        s *= 2
    return v


def _unit_lower_inverse_minus_identity(a):
    n = a.shape[-1]
    row = lax.broadcasted_iota(jnp.int32, (n, n), 0)
    col = lax.broadcasted_iota(jnp.int32, (n, n), 1)
    nmat = -jnp.where(((row // 2 == col // 2) & (row > col))[None], a, 0.0)
    m = 2
    while m < n:
        sel = (row // (2 * m) == col // (2 * m)) & (row % (2 * m) >= m) & (col % (2 * m) < m)
        am = jnp.where(sel[None], a, 0.0)
        nb = nmat.astype(BF16)
        x = am + _bmm(am.astype(BF16), nb)
        y = x + _bmm(nb, x.astype(BF16))
        nmat = nmat - y
        m *= 2
    return nmat


def _gdn_kernel(q_ref, k_ref, v_ref, g_ref, beta_ref, z_ref, gn_ref, y_ref,
                state_ref, pq_ref, nn_ref, au_ref, dl_ref, *, n_chunks):
    @pl.when(pl.program_id(1) == 0)
    def _():
        state_ref[...] = jnp.zeros(state_ref.shape, F32)

    dh = GDN_HEAD_DIM
    scale = dh ** -0.5
    row = lax.broadcasted_iota(jnp.int32, (CHUNK, CHUNK), 0)
    col = lax.broadcasted_iota(jnp.int32, (CHUNK, CHUNK), 1)
    causal = (row >= col)[None]
    strict = (row > col)[None]

    def to_batch(x):
        return jnp.concatenate(
            [x[:, h * dh:(h + 1) * dh].reshape(n_chunks, CHUNK, dh) for h in range(GDN_HEADS)],
            axis=0)

    q = to_batch(q_ref[0])
    k = to_batch(k_ref[0])
    v = to_batch(v_ref[0]).astype(F32)
    beta = to_batch(beta_ref[0])
    gc = to_batch(_chunk_cumsum(g_ref[0]))
    g_last = gc[:, CHUNK - 1:CHUNK, :]
    diff = gc[:, :, :CHUNK] - jnp.swapaxes(gc, 1, 2)[:, :CHUNK, :]
    decay = jnp.exp(jnp.where(causal, diff, MASK_VALUE))
    kq = _bmm_nt(jnp.concatenate([k, q], axis=1), k)
    a = jnp.where(strict, kq[:, :CHUNK] * decay, 0.0) * beta[:, :, :CHUNK]
    nmat = _unit_lower_inverse_minus_identity(a).astype(BF16)
    e_gc = jnp.exp(gc)
    kf = k.astype(F32)
    rhs = jnp.concatenate([v * beta, kf * (beta * e_gc)], axis=2)
    sol = (rhs + _bmm(nmat, rhs.astype(BF16))).astype(BF16)
    aqk = (kq[:, CHUNK:] * decay * scale).astype(BF16)
    kd_t = jnp.swapaxes(kf * jnp.exp(g_last - gc), 1, 2).astype(BF16)
    pn = _bmm(kd_t, sol)
    aq = _bmm(aqk, sol)
    qe = q.astype(F32) * (scale * e_gc) - aq[:, :, dh:]
    pq_ref[...] = jnp.concatenate([pn[:, :, dh:], qe], axis=1).astype(BF16)
    nn_ref[...] = pn[:, :, :dh]
    au_ref[...] = aq[:, :, :dh]
    dl_ref[...] = jnp.broadcast_to(jnp.exp(g_last), dl_ref.shape)

    def recur(c, carry):
        rows = pl.ds(pl.multiple_of(c * CHUNK, CHUNK), CHUNK)
        for h in range(GDN_HEADS):
            lanes = slice(h * dh, (h + 1) * dh)
            bi = h * n_chunks + c
            state = state_ref[h]
            r = _dot(pq_ref[bi], state.astype(BF16))
            state_ref[h] = state * dl_ref[bi][0:1, :] + nn_ref[bi] - r[:dh]
            o = r[dh:] + au_ref[bi]
            zf = z_ref[0, rows, lanes].astype(F32)
            y = _rmsnorm(o, gn_ref[...]) * (zf * _sigmoid(zf))
            y_ref[0, rows, lanes] = y.astype(BF16)
        return carry

    lax.fori_loop(0, n_chunks, recur, 0)


def _gdn(qkv, z, g, beta, gn):
    bn, s, _ = qkv.shape
    tc = min(GDN_TOKEN_TILE, s)
    n_chunks = tc // CHUNK
    nb = GDN_HEADS * n_chunks
    grid = (bn, s // tc)
    col = lambda i: pl.BlockSpec((1, tc, GDN_WIDTH), lambda b, j: (b, j, i))
    return pl.pallas_call(
        functools.partial(_gdn_kernel, n_chunks=n_chunks),
        grid=grid,
        in_specs=[col(0), col(1), col(2), col(0), col(0), col(0),
                  pl.BlockSpec(gn.shape, lambda b, j: (0, 0))],
        out_specs=col(0),
        out_shape=jax.ShapeDtypeStruct((bn, s, GDN_WIDTH), BF16),
        scratch_shapes=[
            pltpu.VMEM((GDN_HEADS, GDN_HEAD_DIM, GDN_HEAD_DIM), F32),
            pltpu.VMEM((nb, GDN_HEAD_DIM + CHUNK, GDN_HEAD_DIM), BF16),
            pltpu.VMEM((nb, GDN_HEAD_DIM, GDN_HEAD_DIM), F32),
            pltpu.VMEM((nb, CHUNK, GDN_HEAD_DIM), F32),
            pltpu.VMEM((nb, SUBLANES, GDN_HEAD_DIM), F32),
        ],
        compiler_params=pltpu.CompilerParams(
            dimension_semantics=("arbitrary", "arbitrary"),
            vmem_limit_bytes=VMEM_LIMIT_BYTES),
        name="gdn",
    )(qkv, qkv, qkv, g, beta, z, gn)


def _attn_kernel(q_ref, kc_ref, kp_ref, vc_ref, vp_ref, bias_ref, o_ref, lse_ref, *, n_sub):
    first_step = pl.program_id(2) == 0
    scale = SWA_HEAD_DIM ** -0.5
    pair_w = 2 * SWA_HEAD_DIM
    lane = lax.broadcasted_iota(jnp.int32, (1, pair_w), 1)
    low = lane < SWA_HEAD_DIM
    lane_full = lax.broadcasted_iota(jnp.int32, (BAND_BLOCK, LANES), 1)
    zero = jnp.zeros((), BF16)

    for sblk in range(n_sub):
        rows = slice(sblk * BAND_BLOCK, (sblk + 1) * BAND_BLOCK)
        prev_rows = slice((sblk - 1) * BAND_BLOCK, sblk * BAND_BLOCK)
        variant = jnp.where(first_step, 1, 0) if sblk == 0 else 0
        lse_tile = jnp.zeros((BAND_BLOCK, LANES), F32)
        for p in range(SWA_HEADS // 2):
            lanes = slice(p * pair_w, (p + 1) * pair_w)
            q = q_ref[0, rows, lanes]
            if sblk == 0:
                k_prev, v_prev = kp_ref[0, :, lanes], vp_ref[0, :, lanes]
            else:
                k_prev, v_prev = kc_ref[0, prev_rows, lanes], vc_ref[0, prev_rows, lanes]
            kcat = jnp.concatenate([k_prev, kc_ref[0, rows, lanes]], axis=0)
            vcat = jnp.concatenate([v_prev, vc_ref[0, rows, lanes]], axis=0)
            outs = []
            for hh in range(2):
                h = 2 * p + hh
                km = jnp.where(low if hh == 0 else jnp.logical_not(low), kcat, zero)
                sc = _dot_nt(q, km) * scale + bias_ref[variant, h]
                m = jnp.max(sc, axis=-1, keepdims=True)
                pr = jnp.exp(sc - m)
                den = jnp.sum(pr, axis=-1, keepdims=True)
                outs.append(_dot(pr.astype(BF16), vcat) / den)
                lse_tile = lse_tile + jnp.where(lane_full == h, m + jnp.log(den), 0.0)
            o_ref[0, rows, lanes] = jnp.where(low, outs[0], outs[1]).astype(BF16)
        lse_ref[0, rows, :] = lse_tile


def _attn_bias(n_back, step):
    qi = np.arange(BAND_BLOCK)[:, None]
    kj = np.arange(2 * BAND_BLOCK)[None, :]
    dist = qi + BAND_BLOCK - kj
    valid = (dist >= 0) & (dist <= n_back)
    slopes = 2.0 ** (-8.0 * np.arange(1, SWA_HEADS + 1) / SWA_HEADS)
    alibi = -slopes[:, None, None] * (dist * step).astype(np.float64)[None]
    regular = np.where(valid[None], alibi, MASK_VALUE)
    first = np.where((valid & (kj >= BAND_BLOCK))[None], alibi, MASK_VALUE)
    return jnp.asarray(np.stack([regular, first]), dtype=F32)


def _attn(view, window, dilation):
    bn, l, _ = view.shape
    qb = min(ATTN_MAX_QUERY_ROWS, l)
    n_sub = qb // BAND_BLOCK
    grid = (bn, dilation, l // qb)
    bias = _attn_bias(window // dilation, dilation)
    cur = lambda part: pl.BlockSpec((1, qb, SWA_WIDTH), lambda b, r, i: (b, i, 3 * r + part))
    prev = lambda part: pl.BlockSpec(
        (1, BAND_BLOCK, SWA_WIDTH), lambda b, r, i: (b, jnp.maximum(i * n_sub - 1, 0), 3 * r + part))
    o, lse = pl.pallas_call(
        functools.partial(_attn_kernel, n_sub=n_sub),
        grid=grid,
        in_specs=[cur(0), cur(1), prev(1), cur(2), prev(2),
                  pl.BlockSpec(bias.shape, lambda b, r, i: (0, 0, 0, 0))],
        out_specs=(pl.BlockSpec((1, qb, SWA_WIDTH), lambda b, r, i: (b, i, r)),
                   pl.BlockSpec((1, qb, LANES), lambda b, r, i: (b, i, r))),
        out_shape=(jax.ShapeDtypeStruct((bn, l, dilation * SWA_WIDTH), BF16),
                   jax.ShapeDtypeStruct((bn, l, dilation * LANES), F32)),
        compiler_params=pltpu.CompilerParams(
            dimension_semantics=("arbitrary", "arbitrary", "arbitrary"),
            vmem_limit_bytes=VMEM_LIMIT_BYTES),
        name=f"attn_d{dilation}",
    )(view, view, view, view, view, bias)
    return o, lse


def _outproj_kernel(x_ref, ya_ref, *rest):
    n_cfg = len(DILATED_CONFIGS)
    o_refs, l_refs = rest[:n_cfg], rest[n_cfg:2 * n_cfg]
    expand_ref, w_ref, out_ref, o_stage, l_stage = rest[2 * n_cfg:]
    tm = x_ref.shape[1]
    n_cb = SWA_WIDTH // LANES
    for gi, (_, dil) in enumerate(DILATED_CONFIGS):
        for r in range(dil):
            rows = pl.ds(r, tm // dil, stride=dil)
            for cb in range(n_cb):
                lo = r * SWA_WIDTH + cb * LANES
                o_stage[gi * n_cb + cb, rows, :] = o_refs[gi][0, :, lo:lo + LANES].astype(F32)
            l_stage[gi, rows, :] = l_refs[gi][0, :, r * LANES:(r + 1) * LANES]
    lses = [l_stage[gi] for gi in range(n_cfg)]
    m = functools.reduce(jnp.maximum, lses)
    es = [jnp.exp(v - m) for v in lses]
    inv = 1.0 / functools.reduce(jnp.add, es)
    yb = None
    for gi, e in enumerate(es):
        wgt = e * inv
        hi = wgt.astype(BF16)
        lo = (wgt - hi.astype(F32)).astype(BF16)
        wide = _dot(hi, expand_ref[...]) + _dot(lo, expand_ref[...])
        o_g = jnp.concatenate([o_stage[gi * n_cb + cb] for cb in range(n_cb)], axis=-1)
        term = wide * o_g
        yb = term if yb is None else yb + term
    y = jnp.concatenate([ya_ref[0], yb.astype(BF16)], axis=-1)
    out_ref[0] = x_ref[0] + _dot(y, w_ref[...])


def _out_proj(x, ya, os_, lses, w_out):
    bn, s, d = x.shape
    tm = min(TOKEN_TILE, s)
    expand = jnp.asarray(
        (np.arange(LANES)[:, None] == (np.arange(SWA_WIDTH)[None, :] // SWA_HEAD_DIM)), dtype=BF16)
    tok = lambda w: pl.BlockSpec((1, tm, w), lambda b, j: (b, j, 0))
    view = lambda w: [pl.BlockSpec((1, tm // dil, dil * w), lambda b, j: (b, j, 0))
                      for _, dil in DILATED_CONFIGS]
    const = lambda a: pl.BlockSpec(a.shape, lambda b, j: (0,) * a.ndim)
    n_cfg = len(DILATED_CONFIGS)
    return pl.pallas_call(
        _outproj_kernel,
        grid=(bn, s // tm),
        in_specs=[tok(d), tok(GDN_WIDTH)] + view(SWA_WIDTH) + view(LANES)
                 + [const(expand), const(w_out)],
        out_specs=tok(d),
        out_shape=jax.ShapeDtypeStruct((bn, s, d), F32),
        scratch_shapes=[pltpu.VMEM((n_cfg * SWA_WIDTH // LANES, tm, LANES), F32),
                        pltpu.VMEM((n_cfg, tm, LANES), F32)],
        compiler_params=pltpu.CompilerParams(
            dimension_semantics=("arbitrary", "arbitrary"),
            vmem_limit_bytes=VMEM_LIMIT_BYTES),
        name="out_proj",
    )(x, ya, *os_, *lses, expand, w_out)


def _ffn_kernel(x_ref, ln_ref, wg_ref, wu_ref, cw_ref, wd_ref, lnf_ref, out_ref, ext_ref,
                *, tm, final_norm):
    first_step = pl.program_id(1) == 0
    x = x_ref[0]
    hb = _rmsnorm(x, ln_ref[...]).astype(BF16)
    gate = _causal_conv_from_ext(ext_ref, cw_ref, _dot(hb, wg_ref[...]), first_step, tm,
                                 FFN_CONV_WIDTH)
    act = (gate * _sigmoid(gate) * _dot(hb, wu_ref[...])).astype(BF16)
    y = x + _dot(act, wd_ref[...])
    if final_norm:
        y = _rmsnorm(y, lnf_ref[...])
    out_ref[0] = y


def _ffn(x, ln, wg, wu, cw, wd, lnf, final_norm):
    bn, s, d = x.shape
    tm = min(FFN_TOKEN_TILE, s)
    tok = pl.BlockSpec((1, tm, d), lambda b, j: (b, j, 0))
    const = lambda a: pl.BlockSpec(a.shape, lambda b, j: (0,) * a.ndim,
                                   pipeline_mode=pl.Buffered(1))
    return pl.pallas_call(
        functools.partial(_ffn_kernel, tm=tm, final_norm=final_norm),
        grid=(bn, s // tm),
        in_specs=[tok, const(ln), const(wg), const(wu), const(cw), const(wd), const(lnf)],
        out_specs=tok,
        out_shape=jax.ShapeDtypeStruct((bn, s, d), F32),
        scratch_shapes=[pltpu.VMEM((tm + SUBLANES, D_FF), F32)],
        compiler_params=pltpu.CompilerParams(
            dimension_semantics=("arbitrary", "arbitrary"),
            vmem_limit_bytes=VMEM_LIMIT_BYTES),
        name="ffn",
    )(x, ln, wg, wu, cw, wd, lnf)


def kernel(x, ln1, w_in, conv_qkv, a_log, dt_bias, gdn_norm, w_out, ln2, w_gate, w_up,
           ffn_conv, w_down, ln_f):
    bn, s, d = x.shape
    depth = w_in.shape[0]
    assert d == D_MODEL and s % GDN_TOKEN_TILE == 0 and s % TOKEN_TILE == 0
    for window, dilation in DILATED_CONFIGS:
        assert window // dilation <= BAND_BLOCK and (s // dilation) % BAND_BLOCK == 0

    c_z = 3 * GDN_WIDTH
    c_b = c_z + GDN_WIDTH
    c_a = c_b + GDN_HEADS
    c_q = c_a + GDN_HEADS
    rep = lambda v: jnp.repeat(v, GDN_HEAD_DIM, axis=-1)
    row = lambda v: v.reshape(1, -1)

    for l in range(depth):
        w = w_in[l]
        wa = w[:, :c_b].astype(BF16)
        wg = jnp.concatenate([rep(w[:, c_a:c_q]), rep(w[:, c_b:c_a])], axis=1).astype(BF16)
        wb = w[:, c_q:].astype(BF16)
        qkv, z, g, beta, *qkvb_views = _in_proj(
            x, row(ln1[l]), wa, wg, wb, conv_qkv[l], row(rep(a_log[l])), row(rep(dt_bias[l])))
        ya = _gdn(qkv, z, g, beta, row(gdn_norm[l]))
        os_, lses = zip(*[_attn(view, window, dilation)
                          for view, (window, dilation) in zip(qkvb_views, DILATED_CONFIGS)])
        x = _out_proj(x, ya, os_, lses, w_out[l].astype(BF16))
        x = _ffn(x, row(ln2[l]), w_gate[l].astype(BF16), w_up[l].astype(BF16), ffn_conv[l],
                 w_down[l].astype(BF16), row(ln_f), final_norm=(l == depth - 1))
    return x
```

```python
import functools

import numpy as np
import jax
import jax.numpy as jnp
from jax import lax
from jax.experimental import pallas as pl
from jax.experimental.pallas import tpu as pltpu

F32 = jnp.float32
BF16 = jnp.bfloat16

D_MODEL = 1024
GDN_HEADS = 4
GDN_HEAD_DIM = 128
GDN_WIDTH = GDN_HEADS * GDN_HEAD_DIM
CONV_WIDTH = 4
CHUNK = 64
SWA_HEADS = 8
SWA_HEAD_DIM = 64
SWA_WIDTH = SWA_HEADS * SWA_HEAD_DIM
DILATED_CONFIGS = ((128, 1), (512, 4), (2048, 16))
BAND_BLOCK = 128
D_FF = 2816
FFN_CONV_WIDTH = 3
RMS_EPS = 1e-6
L2_EPS = 1e-6
MASK_VALUE = -1e30
ATTN_SCALE = SWA_HEAD_DIM ** -0.5
assert ATTN_SCALE == 0.125

LANES = 128
SUBLANES = 8
MXU_COLS = 256
VMEM_LIMIT_BYTES = 56 * 1024 * 1024

TOKEN_TILE = 512
FFN_TOKEN_TILE = 512
GDN_TOKEN_TILE = 256
GDN_SEQS_PER_STEP = 2
ATTN_MAX_QUERY_ROWS = 512


def _sigmoid(v):
    return 1.0 / (1.0 + jnp.exp(-v))


def _dot(a, b):
    return jnp.dot(a, b, preferred_element_type=F32)


def _dot_nt(a, b):
    return lax.dot_general(a, b, (((1,), (1,)), ((), ())), preferred_element_type=F32)


def _rmsnorm(v, gain):
    return v * lax.rsqrt(jnp.mean(v * v, axis=-1, keepdims=True) + RMS_EPS) * gain


def _conv_carry_tail(ext_ref, first_step, tm):
    @pl.when(first_step)
    def _():
        ext_ref[0:SUBLANES, :] = jnp.zeros((SUBLANES, ext_ref.shape[1]), F32)

    @pl.when(jnp.logical_not(first_step))
    def _():
        ext_ref[0:SUBLANES, :] = ext_ref[tm:tm + SUBLANES, :]


def _causal_conv_cols(ext_ref, cw_ref, cur, cols, tm, width):
    ext_ref[SUBLANES:SUBLANES + tm, cols] = cur
    acc = cur * cw_ref[width - 1:width, cols]
    for s in range(1, width):
        acc = acc + ext_ref[pl.ds(SUBLANES - s, tm), cols] * cw_ref[width - 1 - s:width - s, cols]
    return acc


def _deinterleave_plan():
    dils = [dil for _, dil in DILATED_CONFIGS]
    base_of = {}
    for dil in dils:
        cands = [b for b in dils if b < dil and dil % b == 0 and (dil // b) % 8 != 0]
        base_of[dil] = max(cands) if cands else 1
    level = {1: 0}
    for b in sorted(set(base_of[d] for d in dils if d != 1)):
        level.setdefault(b, len(level))
    return [(dil, base_of[dil], level[base_of[dil]], level.get(dil) if dil != 1 else None)
            for dil in dils]


def _inproj_kernel(x_ref, ln_ref, wa_ref, wg_ref, wb_ref, cw_ref, alog_ref, dtb_ref,
                   qkv_ref, z_ref, g_ref, beta_ref, *rest, tm):
    qkvb_refs, (ext_ref, stage_ref) = rest[:-2], rest[-2:]
    first_step = pl.program_id(1) == 0
    hb = _rmsnorm(x_ref[0], ln_ref[...]).astype(BF16)
    _conv_carry_tail(ext_ref, first_step, tm)

    for cb in range(3 * GDN_WIDTH // MXU_COLS):
        cols = slice(cb * MXU_COLS, (cb + 1) * MXU_COLS)
        conv = _causal_conv_cols(ext_ref, cw_ref, _dot(hb, wa_ref[:, cols]), cols, tm, CONV_WIDTH)
        act = conv * _sigmoid(conv)
        for hh in range(MXU_COLS // GDN_HEAD_DIM):
            lo = cb * MXU_COLS + hh * GDN_HEAD_DIM
            blk = act[:, hh * GDN_HEAD_DIM:(hh + 1) * GDN_HEAD_DIM]
            if lo < 2 * GDN_WIDTH:
                blk = blk * lax.rsqrt(jnp.sum(blk * blk, axis=-1, keepdims=True) + L2_EPS)
            qkv_ref[0, :, lo:lo + GDN_HEAD_DIM] = blk.astype(BF16)

    for cb in range(GDN_WIDTH // MXU_COLS):
        cols = slice(cb * MXU_COLS, (cb + 1) * MXU_COLS)
        wcols = slice(3 * GDN_WIDTH + cb * MXU_COLS, 3 * GDN_WIDTH + (cb + 1) * MXU_COLS)
        z_ref[0, :, cols] = _dot(hb, wa_ref[:, wcols]).astype(BF16)

    for cb in range(GDN_WIDTH // MXU_COLS):
        cols = slice(cb * MXU_COLS, (cb + 1) * MXU_COLS)
        a_l = _dot(hb, wg_ref[:, cols]) + dtb_ref[:, cols]
        softplus = jnp.maximum(a_l, 0.0) + jnp.log1p(jnp.exp(-jnp.abs(a_l)))
        g_ref[0, :, cols] = -jnp.exp(alog_ref[:, cols]) * softplus
        wcols = slice(GDN_WIDTH + cb * MXU_COLS, GDN_WIDTH + (cb + 1) * MXU_COLS)
        beta_ref[0, :, cols] = _sigmoid(_dot(hb, wg_ref[:, wcols]))

    width = 3 * SWA_WIDTH
    n_lb = MXU_COLS // LANES
    plan = _deinterleave_plan()
    for cb in range(width // MXU_COLS):
        cols = slice(cb * MXU_COLS, (cb + 1) * MXU_COLS)
        yb = _dot(hb, wb_ref[:, cols])
        for lb in range(n_lb):
            slab = cb * n_lb + lb
            stage_ref[0, slab] = yb[:, lb * LANES:(lb + 1) * LANES]
            for (dil, base, src_level, dst_level), out_ref in zip(plan, qkvb_refs):
                if dil == 1:
                    continue
                step, n_rows = dil // base, tm // dil
                for r in range(dil):
                    start = (r % base) * (tm // base) + r // base
                    part = stage_ref[src_level, slab, pl.ds(start, n_rows, stride=step), :]
                    lo = r * width + cb * MXU_COLS + lb * LANES
                    out_ref[0, :, lo:lo + LANES] = part.astype(BF16)
                    if dst_level is not None:
                        stage_ref[dst_level, slab, r * n_rows:(r + 1) * n_rows, :] = part
        for (dil, _, _, _), out_ref in zip(plan, qkvb_refs):
            if dil == 1:
                out_ref[0, :, cols] = yb.astype(BF16)


def _in_proj(x, ln, wa, wg, wb, cw, alog_x, dtb_x):
    bn, s, d = x.shape
    tm = min(TOKEN_TILE, s)
    grid = (bn, s // tm)
    tok = lambda w: pl.BlockSpec((1, tm, w), lambda b, j: (b, j, 0))
    const = lambda a: pl.BlockSpec(a.shape, lambda b, j: (0,) * a.ndim,
                                   pipeline_mode=pl.Buffered(1))
    dils = [dil for _, dil in DILATED_CONFIGS]
    n_levels = 1 + max(src for _, _, src, _ in _deinterleave_plan())
    out_shape = (
        jax.ShapeDtypeStruct((bn, s, 3 * GDN_WIDTH), BF16),
        jax.ShapeDtypeStruct((bn, s, GDN_WIDTH), BF16),
        jax.ShapeDtypeStruct((bn, s, GDN_WIDTH), F32),
        jax.ShapeDtypeStruct((bn, s, GDN_WIDTH), F32),
    ) + tuple(jax.ShapeDtypeStruct((bn, s // dil, dil * 3 * SWA_WIDTH), BF16) for dil in dils)
    return pl.pallas_call(
        functools.partial(_inproj_kernel, tm=tm),
        grid=grid,
        in_specs=[tok(d), const(ln), const(wa), const(wg), const(wb), const(cw),
                  const(alog_x), const(dtb_x)],
        out_specs=(tok(3 * GDN_WIDTH), tok(GDN_WIDTH), tok(GDN_WIDTH), tok(GDN_WIDTH))
                  + tuple(pl.BlockSpec((1, tm // dil, dil * 3 * SWA_WIDTH), lambda b, j: (b, j, 0))
                          for dil in dils),
        out_shape=out_shape,
        scratch_shapes=[pltpu.VMEM((tm + SUBLANES, 3 * GDN_WIDTH), F32),
                        pltpu.VMEM((n_levels, 3 * SWA_WIDTH // LANES, tm, LANES), F32)],
        compiler_params=pltpu.CompilerParams(
            dimension_semantics=("arbitrary", "arbitrary"),
            vmem_limit_bytes=VMEM_LIMIT_BYTES),
        name="in_proj",
    )(x, ln, wa, wg, wb, cw, alog_x, dtb_x)


def _bmm(a, b):
    return lax.dot_general(a, b, (((2,), (1,)), ((0,), (0,))), preferred_element_type=F32)


def _bmm_nt(a, b):
    return lax.dot_general(a, b, (((2,), (2,)), ((0,), (0,))), preferred_element_type=F32)


def _chunk_cumsum(v):
    row = lax.broadcasted_iota(jnp.int32, v.shape, 0) % CHUNK
    s = 1
    while s < CHUNK:
        v = v + jnp.where(row >= s, pltpu.roll(v, s, axis=0), 0.0)
        s *= 2
    return v


def _unit_lower_inverse_minus_identity(a):
    n = a.shape[-1]
    row = lax.broadcasted_iota(jnp.int32, (n, n), 0)
    col = lax.broadcasted_iota(jnp.int32, (n, n), 1)
    nmat = -jnp.where(((row // 2 == col // 2) & (row > col))[None], a, 0.0)
    m = 2
    while m < n:
        sel = (row // (2 * m) == col // (2 * m)) & (row % (2 * m) >= m) & (col % (2 * m) < m)
        am = jnp.where(sel[None], a, 0.0)
        nb = nmat.astype(BF16)
        x = am + _bmm(am.astype(BF16), nb)
        y = x + _bmm(nb, x.astype(BF16))
        nmat = nmat - y
        m *= 2
    return nmat


def _gdn_kernel(q_ref, k_ref, v_ref, g_ref, beta_ref, z_ref, gn_ref, y_ref,
                state_ref, pq_ref, nn_ref, au_ref, dl_ref, *, n_chunks):
    @pl.when(pl.program_id(1) == 0)
    def _():
        state_ref[...] = jnp.zeros(state_ref.shape, F32)

    dh = GDN_HEAD_DIM
    n_seq = q_ref.shape[0]
    scale = dh ** -0.5
    row = lax.broadcasted_iota(jnp.int32, (CHUNK, CHUNK), 0)
    col = lax.broadcasted_iota(jnp.int32, (CHUNK, CHUNK), 1)
    causal = (row >= col)[None]
    strict = (row > col)[None]

    def to_batch(x):
        return jnp.concatenate(
            [x[b, :, h * dh:(h + 1) * dh].reshape(n_chunks, CHUNK, dh)
             for b in range(n_seq) for h in range(GDN_HEADS)], axis=0)

    q = to_batch(q_ref[...])
    k = to_batch(k_ref[...])
    v = to_batch(v_ref[...]).astype(F32)
    beta = to_batch(beta_ref[...])
    gc = to_batch(jnp.stack([_chunk_cumsum(g_ref[b]) for b in range(n_seq)]))
    g_last = gc[:, CHUNK - 1:CHUNK, :]
    diff = gc[:, :, :CHUNK] - jnp.swapaxes(gc, 1, 2)[:, :CHUNK, :]
    decay = jnp.exp(jnp.where(causal, diff, MASK_VALUE))
    kq = _bmm_nt(jnp.concatenate([k, q], axis=1), k)
    a = jnp.where(strict, kq[:, :CHUNK] * decay, 0.0) * beta[:, :, :CHUNK]
    nmat = _unit_lower_inverse_minus_identity(a).astype(BF16)
    e_gc = jnp.exp(gc)
    kf = k.astype(F32)
    rhs = jnp.concatenate([v * beta, kf * (beta * e_gc)], axis=2)
    sol = (rhs + _bmm(nmat, rhs.astype(BF16))).astype(BF16)
    aqk = (kq[:, CHUNK:] * decay * scale).astype(BF16)
    kd_t = jnp.swapaxes(kf * jnp.exp(g_last - gc), 1, 2).astype(BF16)
    pn = _bmm(kd_t, sol)
    aq = _bmm(aqk, sol)
    qe = q.astype(F32) * (scale * e_gc) - aq[:, :, dh:]
    pq_ref[...] = jnp.concatenate([pn[:, :, dh:], qe], axis=1).astype(BF16)
    nn_ref[...] = pn[:, :, :dh]
    au_ref[...] = aq[:, :, :dh]
    dl_ref[...] = jnp.broadcast_to(jnp.exp(g_last), dl_ref.shape)

    def recur(c, carry):
        rows = pl.ds(pl.multiple_of(c * CHUNK, CHUNK), CHUNK)
        for sh in range(n_seq * GDN_HEADS):
            b, h = divmod(sh, GDN_HEADS)
            lanes = slice(h * dh, (h + 1) * dh)
            bi = sh * n_chunks + c
            state = state_ref[sh]
            r = _dot(pq_ref[bi], state.astype(BF16))
            state_ref[sh] = state * dl_ref[bi][0:1, :] + nn_ref[bi] - r[:dh]
            o = r[dh:] + au_ref[bi]
            zf = z_ref[b, rows, lanes].astype(F32)
            y = _rmsnorm(o, gn_ref[...]) * (zf * _sigmoid(zf))
            y_ref[b, rows, lanes] = y.astype(BF16)
        return carry

    lax.fori_loop(0, n_chunks, recur, 0)


def _gdn(qkv, z, g, beta, gn):
    bn, s, _ = qkv.shape
    tc = min(GDN_TOKEN_TILE, s)
    n_seq = GDN_SEQS_PER_STEP if bn % GDN_SEQS_PER_STEP == 0 else 1
    n_chunks = tc // CHUNK
    nb = n_seq * GDN_HEADS * n_chunks
    grid = (bn // n_seq, s // tc)
    col = lambda i: pl.BlockSpec((n_seq, tc, GDN_WIDTH), lambda b, j: (b, j, i))
    return pl.pallas_call(
        functools.partial(_gdn_kernel, n_chunks=n_chunks),
        grid=grid,
        in_specs=[col(0), col(1), col(2), col(0), col(0), col(0),
                  pl.BlockSpec(gn.shape, lambda b, j: (0, 0))],
        out_specs=col(0),
        out_shape=jax.ShapeDtypeStruct((bn, s, GDN_WIDTH), BF16),
        scratch_shapes=[
            pltpu.VMEM((n_seq * GDN_HEADS, GDN_HEAD_DIM, GDN_HEAD_DIM), F32),
            pltpu.VMEM((nb, GDN_HEAD_DIM + CHUNK, GDN_HEAD_DIM), BF16),
            pltpu.VMEM((nb, GDN_HEAD_DIM, GDN_HEAD_DIM), F32),
            pltpu.VMEM((nb, CHUNK, GDN_HEAD_DIM), F32),
            pltpu.VMEM((nb, SUBLANES, GDN_HEAD_DIM), F32),
        ],
        compiler_params=pltpu.CompilerParams(
            dimension_semantics=("arbitrary", "arbitrary"),
            vmem_limit_bytes=VMEM_LIMIT_BYTES),
        name="gdn",
    )(qkv, qkv, qkv, g, beta, z, gn)


def _attn_kernel(q_ref, kc_ref, kp_ref, vc_ref, vp_ref, bias_ref, o_ref, lse_ref, *, n_sub):
    first_step = pl.program_id(2) == 0
    scale = jnp.asarray(ATTN_SCALE, BF16)
    pair_w = 2 * SWA_HEAD_DIM
    low = lax.broadcasted_iota(jnp.int32, (1, pair_w), 1) < SWA_HEAD_DIM
    lane_full = lax.broadcasted_iota(jnp.int32, (BAND_BLOCK, LANES), 1)
    zero = jnp.zeros((), BF16)
    blk = BAND_BLOCK

    for sblk in range(n_sub):
        rows = slice(sblk * blk, (sblk + 1) * blk)
        prev_rows = slice((sblk - 1) * blk, sblk * blk)
        variant = jnp.where(first_step, 1, 0) if sblk == 0 else 0
        lse_tile = jnp.zeros((blk, LANES), F32)
        for p in range(SWA_HEADS // 2):
            lanes = slice(p * pair_w, (p + 1) * pair_w)
            q = q_ref[0, rows, lanes] * scale
            q2 = jnp.concatenate([jnp.where(low, q, zero), jnp.where(low, zero, q)], axis=0)
            if sblk == 0:
                k_prev, v_prev = kp_ref[0, :, lanes], vp_ref[0, :, lanes]
            else:
                k_prev, v_prev = kc_ref[0, prev_rows, lanes], vc_ref[0, prev_rows, lanes]
            kcat = jnp.concatenate([k_prev, kc_ref[0, rows, lanes]], axis=0)
            vcat = jnp.concatenate([v_prev, vc_ref[0, rows, lanes]], axis=0)
            sc = _dot_nt(q2, kcat) + bias_ref[variant, p]
            m = jnp.max(sc, axis=-1, keepdims=True)
            pr = jnp.exp(sc - m)
            den = jnp.sum(pr, axis=-1, keepdims=True)
            pv = _dot(pr.astype(BF16), vcat) * (1.0 / den)
            o_ref[0, rows, lanes] = jnp.where(low, pv[:blk], pv[blk:]).astype(BF16)
            lse = m + jnp.log(den)
            lse_tile = (lse_tile + jnp.where(lane_full == 2 * p, lse[:blk], 0.0)
                        + jnp.where(lane_full == 2 * p + 1, lse[blk:], 0.0))
        lse_ref[0, rows, :] = lse_tile


def _attn_bias(n_back, step):
    qi = np.arange(BAND_BLOCK)[:, None]
    kj = np.arange(2 * BAND_BLOCK)[None, :]
    dist = qi + BAND_BLOCK - kj
    valid = (dist >= 0) & (dist <= n_back)
    slopes = 2.0 ** (-8.0 * np.arange(1, SWA_HEADS + 1) / SWA_HEADS)
    alibi = -slopes[:, None, None] * (dist * step).astype(np.float64)[None]
    regular = np.where(valid[None], alibi, MASK_VALUE)
    first = np.where((valid & (kj >= BAND_BLOCK))[None], alibi, MASK_VALUE)
    bias = np.stack([regular, first])
    return jnp.asarray(bias.reshape(2, SWA_HEADS // 2, 2 * BAND_BLOCK, 2 * BAND_BLOCK), dtype=F32)


def _attn(view, window, dilation):
    bn, l, _ = view.shape
    qb = min(ATTN_MAX_QUERY_ROWS, l)
    n_sub = qb // BAND_BLOCK
    grid = (bn, dilation, l // qb)
    bias = _attn_bias(window // dilation, dilation)
    cur = lambda part: pl.BlockSpec((1, qb, SWA_WIDTH), lambda b, r, i: (b, i, 3 * r + part))
    prev = lambda part: pl.BlockSpec(
        (1, BAND_BLOCK, SWA_WIDTH), lambda b, r, i: (b, jnp.maximum(i * n_sub - 1, 0), 3 * r + part))
    o, lse = pl.pallas_call(
        functools.partial(_attn_kernel, n_sub=n_sub),
        grid=grid,
        in_specs=[cur(0), cur(1), prev(1), cur(2), prev(2),
                  pl.BlockSpec(bias.shape, lambda b, r, i: (0, 0, 0, 0))],
        out_specs=(pl.BlockSpec((1, qb, SWA_WIDTH), lambda b, r, i: (b, i, r)),
                   pl.BlockSpec((1, qb, LANES), lambda b, r, i: (b, i, r))),
        out_shape=(jax.ShapeDtypeStruct((bn, l, dilation * SWA_WIDTH), BF16),
                   jax.ShapeDtypeStruct((bn, l, dilation * LANES), F32)),
        compiler_params=pltpu.CompilerParams(
            dimension_semantics=("arbitrary", "arbitrary", "arbitrary"),
            vmem_limit_bytes=VMEM_LIMIT_BYTES),
        name=f"attn_d{dilation}",
    )(view, view, view, view, view, bias)
    return o, lse


def _outproj_kernel(x_ref, ya_ref, *rest):
    n_cfg = len(DILATED_CONFIGS)
    o_refs, l_refs = rest[:n_cfg], rest[n_cfg:2 * n_cfg]
    expand_ref, w_ref, out_ref, o_stage, l_stage = rest[2 * n_cfg:]
    tm = x_ref.shape[1]
    n_cb = SWA_WIDTH // LANES
    for gi, (_, dil) in enumerate(DILATED_CONFIGS):
        for r in range(dil):
            rows = pl.ds(r, tm // dil, stride=dil)
            for cb in range(n_cb):
                lo = r * SWA_WIDTH + cb * LANES
                o_stage[gi * n_cb + cb, rows, :] = o_refs[gi][0, :, lo:lo + LANES].astype(F32)
            l_stage[gi, rows, :] = l_refs[gi][0, :, r * LANES:(r + 1) * LANES]
    lses = [l_stage[gi] for gi in range(n_cfg)]
    m = functools.reduce(jnp.maximum, lses)
    es = [jnp.exp(v - m) for v in lses]
    inv = 1.0 / functools.reduce(jnp.add, es)
    yb = None
    for gi, e in enumerate(es):
        wgt = e * inv
        hi = wgt.astype(BF16)
        lo = (wgt - hi.astype(F32)).astype(BF16)
        wide = _dot(jnp.concatenate([hi, lo], axis=-1), expand_ref[...])
        o_g = jnp.concatenate([o_stage[gi * n_cb + cb] for cb in range(n_cb)], axis=-1)
        term = wide * o_g
        yb = term if yb is None else yb + term
    y = jnp.concatenate([ya_ref[0], yb.astype(BF16)], axis=-1)
    out_ref[0] = x_ref[0] + _dot(y, w_ref[...])


def _out_proj(x, ya, os_, lses, w_out):
    bn, s, d = x.shape
    tm = min(TOKEN_TILE, s)
    expand = jnp.asarray(np.tile(
        np.arange(LANES)[:, None] == (np.arange(SWA_WIDTH)[None, :] // SWA_HEAD_DIM), (2, 1)),
        dtype=BF16)
    tok = lambda w: pl.BlockSpec((1, tm, w), lambda b, j: (b, j, 0))
    view = lambda w: [pl.BlockSpec((1, tm // dil, dil * w), lambda b, j: (b, j, 0))
                      for _, dil in DILATED_CONFIGS]
    const = lambda a: pl.BlockSpec(a.shape, lambda b, j: (0,) * a.ndim,
                                   pipeline_mode=pl.Buffered(1))
    n_cfg = len(DILATED_CONFIGS)
    return pl.pallas_call(
        _outproj_kernel,
        grid=(bn, s // tm),
        in_specs=[tok(d), tok(GDN_WIDTH)] + view(SWA_WIDTH) + view(LANES)
                 + [const(expand), const(w_out)],
        out_specs=tok(d),
        out_shape=jax.ShapeDtypeStruct((bn, s, d), F32),
        scratch_shapes=[pltpu.VMEM((n_cfg * SWA_WIDTH // LANES, tm, LANES), F32),
                        pltpu.VMEM((n_cfg, tm, LANES), F32)],
        compiler_params=pltpu.CompilerParams(
            dimension_semantics=("arbitrary", "arbitrary"),
            vmem_limit_bytes=VMEM_LIMIT_BYTES),
        name="out_proj",
    )(x, ya, *os_, *lses, expand, w_out)


def _ffn_kernel(x_ref, ln_ref, wg_ref, wu_ref, cw_ref, wd_ref, lnf_ref, out_ref, ext_ref, act_ref,
                *, tm, final_norm):
    first_step = pl.program_id(1) == 0
    x = x_ref[0]
    hb = _rmsnorm(x, ln_ref[...]).astype(BF16)
    _conv_carry_tail(ext_ref, first_step, tm)
    for cb in range(D_FF // MXU_COLS):
        cols = slice(cb * MXU_COLS, (cb + 1) * MXU_COLS)
        gate = _causal_conv_cols(ext_ref, cw_ref, _dot(hb, wg_ref[:, cols]), cols, tm,
                                 FFN_CONV_WIDTH)
        act_ref[:, cols] = (gate * _sigmoid(gate) * _dot(hb, wu_ref[:, cols])).astype(BF16)
    y = x + _dot(act_ref[...], wd_ref[...])
    if final_norm:
        y = _rmsnorm(y, lnf_ref[...])
    out_ref[0] = y


def _ffn(x, ln, wg, wu, cw, wd, lnf, final_norm):
    bn, s, d = x.shape
    tm = min(FFN_TOKEN_TILE, s)
    tok = pl.BlockSpec((1, tm, d), lambda b, j: (b, j, 0))
    const = lambda a: pl.BlockSpec(a.shape, lambda b, j: (0,) * a.ndim,
                                   pipeline_mode=pl.Buffered(1))
    return pl.pallas_call(
        functools.partial(_ffn_kernel, tm=tm, final_norm=final_norm),
        grid=(bn, s // tm),
        in_specs=[tok, const(ln), const(wg), const(wu), const(cw), const(wd), const(lnf)],
        out_specs=tok,
        out_shape=jax.ShapeDtypeStruct((bn, s, d), F32),
        scratch_shapes=[pltpu.VMEM((tm + SUBLANES, D_FF), F32),
                        pltpu.VMEM((tm, D_FF), BF16)],
        compiler_params=pltpu.CompilerParams(
            dimension_semantics=("arbitrary", "arbitrary"),
            vmem_limit_bytes=VMEM_LIMIT_BYTES),
        name="ffn",
    )(x, ln, wg, wu, cw, wd, lnf)


def kernel(x, ln1, w_in, conv_qkv, a_log, dt_bias, gdn_norm, w_out, ln2, w_gate, w_up,
           ffn_conv, w_down, ln_f):
    bn, s, d = x.shape
    depth = w_in.shape[0]
    assert d == D_MODEL and s % GDN_TOKEN_TILE == 0 and s % TOKEN_TILE == 0
    for window, dilation in DILATED_CONFIGS:
        assert window // dilation <= BAND_BLOCK and (s // dilation) % BAND_BLOCK == 0

    c_z = 3 * GDN_WIDTH
    c_b = c_z + GDN_WIDTH
    c_a = c_b + GDN_HEADS
    c_q = c_a + GDN_HEADS
    rep = lambda v: jnp.repeat(v, GDN_HEAD_DIM, axis=-1)
    row = lambda v: v.reshape(1, -1)

    for l in range(depth):
        w = w_in[l]
        wa = w[:, :c_b].astype(BF16)
        wg = jnp.concatenate([rep(w[:, c_a:c_q]), rep(w[:, c_b:c_a])], axis=1).astype(BF16)
        wb = w[:, c_q:].astype(BF16)
        qkv, z, g, beta, *qkvb_views = _in_proj(
            x, row(ln1[l]), wa, wg, wb, conv_qkv[l], row(rep(a_log[l])), row(rep(dt_bias[l])))
        ya = _gdn(qkv, z, g, beta, row(gdn_norm[l]))
        os_, lses = zip(*[_attn(view, window, dilation)
                          for view, (window, dilation) in zip(qkvb_views, DILATED_CONFIGS)])
        x = _out_proj(x, ya, os_, lses, w_out[l].astype(BF16))
        x = _ffn(x, row(ln2[l]), w_gate[l].astype(BF16), w_up[l].astype(BF16), ffn_conv[l],
                 w_down[l].astype(BF16), row(ln_f), final_norm=(l == depth - 1))
    return x
```

```python
import functools

import numpy as np
import jax
import jax.numpy as jnp
from jax import lax
from jax.experimental import pallas as pl
from jax.experimental.pallas import tpu as pltpu

F32 = jnp.float32
BF16 = jnp.bfloat16

D_MODEL = 1024
GDN_HEADS = 4
GDN_HEAD_DIM = 128
GDN_WIDTH = GDN_HEADS * GDN_HEAD_DIM
CONV_WIDTH = 4
CHUNK = 64
SWA_HEADS = 8
SWA_HEAD_DIM = 64
SWA_WIDTH = SWA_HEADS * SWA_HEAD_DIM
DILATED_CONFIGS = ((128, 1), (512, 4), (2048, 16))
BAND_BLOCK = 128
D_FF = 2816
FFN_CONV_WIDTH = 3
RMS_EPS = 1e-6
L2_EPS = 1e-6
MASK_VALUE = -1e30
ATTN_SCALE = SWA_HEAD_DIM ** -0.5
assert ATTN_SCALE == 0.125

LANES = 128
SUBLANES = 8
MXU_COLS = 256
VMEM_LIMIT_BYTES = 56 * 1024 * 1024

TOKEN_TILE = 512
GDN_TOKEN_TILE = 256
GDN_SEQS_PER_STEP = 2
ATTN_MAX_QUERY_ROWS = 512
QUERY_SUB = 64
ATTN_PIPELINE_LAG = 6
ATTN_LSE_HEAD_ORDER = tuple(range(0, SWA_HEADS, 2)) + tuple(range(1, SWA_HEADS, 2))


def _sigmoid(v):
    return 1.0 / (1.0 + jnp.exp(-v))


def _dot(a, b):
    return jnp.dot(a, b, preferred_element_type=F32)


def _dot_nt(a, b):
    return lax.dot_general(a, b, (((1,), (1,)), ((), ())), preferred_element_type=F32)


def _rmsnorm(v, gain):
    return v * lax.rsqrt(jnp.mean(v * v, axis=-1, keepdims=True) + RMS_EPS) * gain


def _conv_carry_tail(ext_ref, first_step, tm):
    @pl.when(first_step)
    def _():
        ext_ref[0:SUBLANES, :] = jnp.zeros((SUBLANES, ext_ref.shape[1]), F32)

    @pl.when(jnp.logical_not(first_step))
    def _():
        ext_ref[0:SUBLANES, :] = ext_ref[tm:tm + SUBLANES, :]


def _causal_conv_cols(ext_ref, cw_ref, cur, cols, tm, width):
    ext_ref[SUBLANES:SUBLANES + tm, cols] = cur
    acc = cur * cw_ref[width - 1:width, cols]
    for s in range(1, width):
        acc = acc + ext_ref[pl.ds(SUBLANES - s, tm), cols] * cw_ref[width - 1 - s:width - s, cols]
    return acc


def _deinterleave_plan():
    dils = [dil for _, dil in DILATED_CONFIGS]
    base_of = {}
    for dil in dils:
        cands = [b for b in dils if b < dil and dil % b == 0 and (dil // b) % 8 != 0]
        base_of[dil] = max(cands) if cands else 1
    level = {1: 0}
    for b in sorted(set(base_of[d] for d in dils if d != 1)):
        level.setdefault(b, len(level))
    return [(dil, base_of[dil], level[base_of[dil]], level.get(dil) if dil != 1 else None)
            for dil in dils]


def _inproj_kernel(x_ref, ln_ref, wa_ref, wg_ref, wb_ref, cw_ref, alog_ref, dtb_ref,
                   qkv_ref, z_ref, g_ref, beta_ref, *rest, tm):
    qkvb_refs, (ext_ref, stage_ref) = rest[:-2], rest[-2:]
    first_step = pl.program_id(1) == 0
    hb = _rmsnorm(x_ref[0], ln_ref[...]).astype(BF16)
    _conv_carry_tail(ext_ref, first_step, tm)

    def gdn_qkv_block(cb):
        cols = slice(cb * MXU_COLS, (cb + 1) * MXU_COLS)
        conv = _causal_conv_cols(ext_ref, cw_ref, _dot(hb, wa_ref[:, cols]), cols, tm, CONV_WIDTH)
        act = conv * _sigmoid(conv)
        for hh in range(MXU_COLS // GDN_HEAD_DIM):
            lo = cb * MXU_COLS + hh * GDN_HEAD_DIM
            blk = act[:, hh * GDN_HEAD_DIM:(hh + 1) * GDN_HEAD_DIM]
            if lo < 2 * GDN_WIDTH:
                blk = blk * lax.rsqrt(jnp.sum(blk * blk, axis=-1, keepdims=True) + L2_EPS)
            qkv_ref[0, :, lo:lo + GDN_HEAD_DIM] = blk.astype(BF16)

    def gate_block(cb):
        cols = slice(cb * MXU_COLS, (cb + 1) * MXU_COLS)
        wcols = slice(3 * GDN_WIDTH + cb * MXU_COLS, 3 * GDN_WIDTH + (cb + 1) * MXU_COLS)
        z_ref[0, :, cols] = _dot(hb, wa_ref[:, wcols]).astype(BF16)

    def decay_block(cb):
        cols = slice(cb * MXU_COLS, (cb + 1) * MXU_COLS)
        a_l = _dot(hb, wg_ref[:, cols]) + dtb_ref[:, cols]
        softplus = jnp.maximum(a_l, 0.0) + jnp.log1p(jnp.exp(-jnp.abs(a_l)))
        g_ref[0, :, cols] = -jnp.exp(alog_ref[:, cols]) * softplus

    def beta_block(cb):
        cols = slice(cb * MXU_COLS, (cb + 1) * MXU_COLS)
        wcols = slice(GDN_WIDTH + cb * MXU_COLS, GDN_WIDTH + (cb + 1) * MXU_COLS)
        beta_ref[0, :, cols] = _sigmoid(_dot(hb, wg_ref[:, wcols]))

    width = 3 * SWA_WIDTH
    n_lb = MXU_COLS // LANES
    plan = _deinterleave_plan()

    def attn_qkv_block(cb):
        cols = slice(cb * MXU_COLS, (cb + 1) * MXU_COLS)
        yb = _dot(hb, wb_ref[:, cols])
        for lb in range(n_lb):
            slab = cb * n_lb + lb
            stage_ref[0, slab] = yb[:, lb * LANES:(lb + 1) * LANES]
            for (dil, base, src_level, dst_level), out_ref in zip(plan, qkvb_refs):
                if dil == 1:
                    continue
                step, n_rows = dil // base, tm // dil
                for r in range(dil):
                    start = (r % base) * (tm // base) + r // base
                    part = stage_ref[src_level, slab, pl.ds(start, n_rows, stride=step), :]
                    lo = r * width + cb * MXU_COLS + lb * LANES
                    out_ref[0, :, lo:lo + LANES] = part.astype(BF16)
                    if dst_level is not None:
                        stage_ref[dst_level, slab, r * n_rows:(r + 1) * n_rows, :] = part
        for (dil, _, _, _), out_ref in zip(plan, qkvb_refs):
            if dil == 1:
                out_ref[0, :, cols] = yb.astype(BF16)

    heavy = [functools.partial(gdn_qkv_block, cb) for cb in range(3 * GDN_WIDTH // MXU_COLS)]
    light = [functools.partial(attn_qkv_block, cb) for cb in range(width // MXU_COLS)]
    for cb in range(GDN_WIDTH // MXU_COLS):
        light += [functools.partial(gate_block, cb), functools.partial(decay_block, cb),
                  functools.partial(beta_block, cb)]
    n_light_per_heavy = len(light) // len(heavy)
    for i, heavy_block in enumerate(heavy):
        heavy_block()
        for light_block in light[i * n_light_per_heavy:(i + 1) * n_light_per_heavy]:
            light_block()
    for light_block in light[len(heavy) * n_light_per_heavy:]:
        light_block()


def _in_proj(x, ln, wa, wg, wb, cw, alog_x, dtb_x):
    bn, s, d = x.shape
    tm = min(TOKEN_TILE, s)
    grid = (bn, s // tm)
    tok = lambda w: pl.BlockSpec((1, tm, w), lambda b, j: (b, j, 0))
    const = lambda a: pl.BlockSpec(a.shape, lambda b, j: (0,) * a.ndim,
                                   pipeline_mode=pl.Buffered(1))
    dils = [dil for _, dil in DILATED_CONFIGS]
    n_levels = 1 + max(src for _, _, src, _ in _deinterleave_plan())
    out_shape = (
        jax.ShapeDtypeStruct((bn, s, 3 * GDN_WIDTH), BF16),
        jax.ShapeDtypeStruct((bn, s, GDN_WIDTH), BF16),
        jax.ShapeDtypeStruct((bn, s, GDN_WIDTH), F32),
        jax.ShapeDtypeStruct((bn, s, GDN_WIDTH), F32),
    ) + tuple(jax.ShapeDtypeStruct((bn, s // dil, dil * 3 * SWA_WIDTH), BF16) for dil in dils)
    return pl.pallas_call(
        functools.partial(_inproj_kernel, tm=tm),
        grid=grid,
        in_specs=[tok(d), const(ln), const(wa), const(wg), const(wb), const(cw),
                  const(alog_x), const(dtb_x)],
        out_specs=(tok(3 * GDN_WIDTH), tok(GDN_WIDTH), tok(GDN_WIDTH), tok(GDN_WIDTH))
                  + tuple(pl.BlockSpec((1, tm // dil, dil * 3 * SWA_WIDTH), lambda b, j: (b, j, 0))
                          for dil in dils),
        out_shape=out_shape,
        scratch_shapes=[pltpu.VMEM((tm + SUBLANES, 3 * GDN_WIDTH), F32),
                        pltpu.VMEM((n_levels, 3 * SWA_WIDTH // LANES, tm, LANES), F32)],
        compiler_params=pltpu.CompilerParams(
            dimension_semantics=("arbitrary", "arbitrary"),
            vmem_limit_bytes=VMEM_LIMIT_BYTES),
        name="in_proj",
    )(x, ln, wa, wg, wb, cw, alog_x, dtb_x)


def _bmm(a, b):
    return lax.dot_general(a, b, (((2,), (1,)), ((0,), (0,))), preferred_element_type=F32)


def _bmm_nt(a, b):
    return lax.dot_general(a, b, (((2,), (2,)), ((0,), (0,))), preferred_element_type=F32)


def _chunk_cumsum(v):
    row = lax.broadcasted_iota(jnp.int32, v.shape, 0) % CHUNK
    s = 1
    while s < CHUNK:
        v = v + jnp.where(row >= s, pltpu.roll(v, s, axis=0), 0.0)
        s *= 2
    return v


def _unit_lower_inverse_minus_identity(a):
    n = a.shape[-1]
    row = lax.broadcasted_iota(jnp.int32, (n, n), 0)
    col = lax.broadcasted_iota(jnp.int32, (n, n), 1)
    nmat = -jnp.where(((row // 2 == col // 2) & (row > col))[None], a, 0.0)
    m = 2
    while m < n:
        sel = (row // (2 * m) == col // (2 * m)) & (row % (2 * m) >= m) & (col % (2 * m) < m)
        am = jnp.where(sel[None], a, 0.0)
        nb = nmat.astype(BF16)
        x = am + _bmm(am.astype(BF16), nb)
        y = x + _bmm(nb, x.astype(BF16))
        nmat = nmat - y
        m *= 2
    return nmat


def _gdn_kernel(q_ref, k_ref, v_ref, g_ref, beta_ref, z_ref, gn_ref, y_ref,
                state_ref, pq_ref, nn_ref, au_ref, dl_ref, *, n_chunks):
    @pl.when(pl.program_id(1) == 0)
    def _():
        state_ref[...] = jnp.zeros(state_ref.shape, F32)

    dh = GDN_HEAD_DIM
    n_seq = q_ref.shape[0]
    scale = dh ** -0.5
    row = lax.broadcasted_iota(jnp.int32, (CHUNK, CHUNK), 0)
    col = lax.broadcasted_iota(jnp.int32, (CHUNK, CHUNK), 1)
    causal = (row >= col)[None]
    strict = (row > col)[None]

    def to_batch(x):
        return jnp.concatenate(
            [x[b, :, h * dh:(h + 1) * dh].reshape(n_chunks, CHUNK, dh)
             for b in range(n_seq) for h in range(GDN_HEADS)], axis=0)

    q = to_batch(q_ref[...])
    k = to_batch(k_ref[...])
    v = to_batch(v_ref[...]).astype(F32)
    beta = to_batch(beta_ref[...])
    gc = to_batch(jnp.stack([_chunk_cumsum(g_ref[b]) for b in range(n_seq)]))
    g_last = gc[:, CHUNK - 1:CHUNK, :]
    diff = gc[:, :, :CHUNK] - jnp.swapaxes(gc, 1, 2)[:, :CHUNK, :]
    decay = jnp.exp(jnp.where(causal, diff, MASK_VALUE))
    kq = _bmm_nt(jnp.concatenate([k, q], axis=1), k)
    a = jnp.where(strict, kq[:, :CHUNK] * decay, 0.0) * beta[:, :, :CHUNK]
    nmat = _unit_lower_inverse_minus_identity(a).astype(BF16)
    e_gc = jnp.exp(gc)
    kf = k.astype(F32)
    rhs = jnp.concatenate([v * beta, kf * (beta * e_gc)], axis=2)
    sol = (rhs + _bmm(nmat, rhs.astype(BF16))).astype(BF16)
    aqk = (kq[:, CHUNK:] * decay * scale).astype(BF16)
    kd_t = jnp.swapaxes(kf * jnp.exp(g_last - gc), 1, 2).astype(BF16)
    pn = _bmm(kd_t, sol)
    aq = _bmm(aqk, sol)
    qe = q.astype(F32) * (scale * e_gc) - aq[:, :, dh:]
    pq_ref[...] = jnp.concatenate([pn[:, :, dh:], qe], axis=1).astype(BF16)
    nn_ref[...] = pn[:, :, :dh]
    au_ref[...] = aq[:, :, :dh]
    dl_ref[...] = jnp.broadcast_to(jnp.exp(g_last), dl_ref.shape)

    def recur(c, carry):
        rows = pl.ds(pl.multiple_of(c * CHUNK, CHUNK), CHUNK)
        for sh in range(n_seq * GDN_HEADS):
            b, h = divmod(sh, GDN_HEADS)
            lanes = slice(h * dh, (h + 1) * dh)
            bi = sh * n_chunks + c
            state = state_ref[sh]
            r = _dot(pq_ref[bi], state.astype(BF16))
            state_ref[sh] = state * dl_ref[bi][0:1, :] + nn_ref[bi] - r[:dh]
            o = r[dh:] + au_ref[bi]
            zf = z_ref[b, rows, lanes].astype(F32)
            y = _rmsnorm(o, gn_ref[...]) * (zf * _sigmoid(zf))
            y_ref[b, rows, lanes] = y.astype(BF16)
        return carry

    lax.fori_loop(0, n_chunks, recur, 0)


def _gdn(qkv, z, g, beta, gn):
    bn, s, _ = qkv.shape
    tc = min(GDN_TOKEN_TILE, s)
    n_seq = GDN_SEQS_PER_STEP if bn % GDN_SEQS_PER_STEP == 0 else 1
    n_chunks = tc // CHUNK
    nb = n_seq * GDN_HEADS * n_chunks
    grid = (bn // n_seq, s // tc)
    col = lambda i: pl.BlockSpec((n_seq, tc, GDN_WIDTH), lambda b, j: (b, j, i))
    return pl.pallas_call(
        functools.partial(_gdn_kernel, n_chunks=n_chunks),
        grid=grid,
        in_specs=[col(0), col(1), col(2), col(0), col(0), col(0),
                  pl.BlockSpec(gn.shape, lambda b, j: (0, 0))],
        out_specs=col(0),
        out_shape=jax.ShapeDtypeStruct((bn, s, GDN_WIDTH), BF16),
        scratch_shapes=[
            pltpu.VMEM((n_seq * GDN_HEADS, GDN_HEAD_DIM, GDN_HEAD_DIM), F32),
            pltpu.VMEM((nb, GDN_HEAD_DIM + CHUNK, GDN_HEAD_DIM), BF16),
            pltpu.VMEM((nb, GDN_HEAD_DIM, GDN_HEAD_DIM), F32),
            pltpu.VMEM((nb, CHUNK, GDN_HEAD_DIM), F32),
            pltpu.VMEM((nb, SUBLANES, GDN_HEAD_DIM), F32),
        ],
        compiler_params=pltpu.CompilerParams(
            dimension_semantics=("arbitrary", "arbitrary"),
            vmem_limit_bytes=VMEM_LIMIT_BYTES),
        name="gdn",
    )(qkv, qkv, qkv, g, beta, z, gn)


def _attn_kernel(q_ref, kc_ref, kp_ref, vc_ref, vp_ref, bias_ref, o_ref, lse_ref, *, n_sub):
    first_step = pl.program_id(2) == 0
    scale = jnp.asarray(ATTN_SCALE, BF16)
    pair_w = 2 * SWA_HEAD_DIM
    n_pairs = SWA_HEADS // 2
    low = lax.broadcasted_iota(jnp.int32, (1, pair_w), 1) < SWA_HEAD_DIM
    zero = jnp.zeros((), BF16)
    blk = BAND_BLOCK
    n_qs = blk // QUERY_SUB
    combos = [(b, p, s) for b in range(n_sub) for p in range(n_pairs) for s in range(n_qs)]

    def window(cur_ref, prev_ref, b, p, s):
        lanes = slice(p * pair_w, (p + 1) * pair_w)
        lo = b * blk + s * QUERY_SUB - blk
        if lo >= 0:
            return cur_ref[0, lo:lo + blk + QUERY_SUB, lanes]
        return jnp.concatenate([prev_ref[0, blk + lo:blk, lanes],
                                cur_ref[0, 0:lo + blk + QUERY_SUB, lanes]], axis=0)

    def q_rows(b, s):
        return slice(b * blk + s * QUERY_SUB, b * blk + (s + 1) * QUERY_SUB)

    scores, stats = {}, {}

    def score_stage(c):
        b, p, s = c
        q = q_ref[0, q_rows(b, s), p * pair_w:(p + 1) * pair_w] * scale
        q2 = jnp.concatenate([jnp.where(low, q, zero), jnp.where(low, zero, q)], axis=0)
        variant = jnp.where(first_step, 1 + s, 0) if b == 0 else 0
        scores[c] = _dot_nt(window(kc_ref, kp_ref, b, p, s), q2) + bias_ref[variant, p]

    def softmax_stage(c):
        st = scores.pop(c)
        m = jnp.max(st, axis=0, keepdims=True)
        pt = jnp.exp(st - m)
        den = jnp.sum(pt, axis=0, keepdims=True)
        stats[c] = ((pt * (1.0 / den)).astype(BF16), m + jnp.log(den))

    def output_stage(c):
        b, p, s = c
        pv = lax.dot_general(stats[c][0], window(vc_ref, vp_ref, b, p, s),
                             (((0,), (0,)), ((), ())), preferred_element_type=F32)
        o_ref[0, q_rows(b, s), p * pair_w:(p + 1) * pair_w] = jnp.where(
            low, pv[:QUERY_SUB], pv[QUERY_SUB:]).astype(BF16)

    stages = (score_stage, softmax_stage, output_stage)
    for t in range(len(combos) + (len(stages) - 1) * ATTN_PIPELINE_LAG):
        for j, stage in enumerate(stages):
            i = t - j * ATTN_PIPELINE_LAG
            if 0 <= i < len(combos):
                stage(combos[i])

    fill = jnp.zeros((LANES - 2 * n_pairs, pair_w), F32)
    for b in range(n_sub):
        for s in range(n_qs):
            rows = [stats[b, p, s][1] for p in range(n_pairs)]
            rows += [pltpu.roll(r, SWA_HEAD_DIM, axis=1) for r in rows]
            tile = jnp.concatenate(rows + [fill], axis=0).T
            lse_ref[0, q_rows(b, s), :] = tile[:QUERY_SUB]


def _attn_bias(n_back, step):
    n_qs = BAND_BLOCK // QUERY_SUB
    kj = np.arange(BAND_BLOCK + QUERY_SUB)[:, None]
    qi = np.arange(QUERY_SUB)[None, :]
    dist = BAND_BLOCK + qi - kj
    valid = (dist >= 0) & (dist <= n_back)
    slopes = 2.0 ** (-8.0 * np.arange(1, SWA_HEADS + 1) / SWA_HEADS)
    alibi = -slopes[:, None, None] * (dist * step).astype(np.float64)[None]
    variants = [np.where(valid[None], alibi, MASK_VALUE)]
    for s in range(n_qs):
        exists = kj + s * QUERY_SUB >= BAND_BLOCK
        variants.append(np.where((valid & exists)[None], alibi, MASK_VALUE))
    bias = np.stack(variants)
    bias = bias.reshape(len(variants), SWA_HEADS // 2, 2, BAND_BLOCK + QUERY_SUB, QUERY_SUB)
    bias = bias.transpose(0, 1, 3, 2, 4).reshape(
        len(variants), SWA_HEADS // 2, BAND_BLOCK + QUERY_SUB, 2 * QUERY_SUB)
    return jnp.asarray(bias, dtype=F32)


def _attn(view, window, dilation):
    bn, l, _ = view.shape
    qb = min(ATTN_MAX_QUERY_ROWS, l)
    n_sub = qb // BAND_BLOCK
    grid = (bn, dilation, l // qb)
    bias = _attn_bias(window // dilation, dilation)
    cur = lambda part: pl.BlockSpec((1, qb, SWA_WIDTH), lambda b, r, i: (b, i, 3 * r + part))
    prev = lambda part: pl.BlockSpec(
        (1, BAND_BLOCK, SWA_WIDTH), lambda b, r, i: (b, jnp.maximum(i * n_sub - 1, 0), 3 * r + part))
    o, lse = pl.pallas_call(
        functools.partial(_attn_kernel, n_sub=n_sub),
        grid=grid,
        in_specs=[cur(0), cur(1), prev(1), cur(2), prev(2),
                  pl.BlockSpec(bias.shape, lambda b, r, i: (0, 0, 0, 0))],
        out_specs=(pl.BlockSpec((1, qb, SWA_WIDTH), lambda b, r, i: (b, i, r)),
                   pl.BlockSpec((1, qb, LANES), lambda b, r, i: (b, i, r))),
        out_shape=(jax.ShapeDtypeStruct((bn, l, dilation * SWA_WIDTH), BF16),
                   jax.ShapeDtypeStruct((bn, l, dilation * LANES), F32)),
        compiler_params=pltpu.CompilerParams(
            dimension_semantics=("arbitrary", "arbitrary", "arbitrary"),
            vmem_limit_bytes=VMEM_LIMIT_BYTES),
        name=f"attn_d{dilation}",
    )(view, view, view, view, view, bias)
    return o, lse


def _mix_ffn_kernel(x_ref, ya_ref, *rest, tm, final_norm):
    n_cfg = len(DILATED_CONFIGS)
    o_refs, l_refs = rest[:n_cfg], rest[n_cfg:2 * n_cfg]
    (expand_ref, wo_ref, ln_ref, wg_ref, wu_ref, cw_ref, wd_ref, lnf_ref, out_ref,
     o_stage, l_stage, ext_ref, act_ref) = rest[2 * n_cfg:]
    first_step = pl.program_id(1) == 0
    _conv_carry_tail(ext_ref, first_step, tm)

    n_cb = SWA_WIDTH // LANES
    for gi, (_, dil) in enumerate(DILATED_CONFIGS):
        for r in range(dil):
            rows = pl.ds(r, tm // dil, stride=dil)
            for cb in range(n_cb):
                lo = r * SWA_WIDTH + cb * LANES
                o_stage[gi * n_cb + cb, rows, :] = o_refs[gi][0, :, lo:lo + LANES].astype(F32)
            l_stage[gi, rows, :] = l_refs[gi][0, :, r * LANES:(r + 1) * LANES]
    lses = [l_stage[gi] for gi in range(n_cfg)]
    m = functools.reduce(jnp.maximum, lses)
    es = [jnp.exp(v - m) for v in lses]
    inv = 1.0 / functools.reduce(jnp.add, es)
    yb = None
    for gi, e in enumerate(es):
        wgt = e * inv
        hi = wgt.astype(BF16)
        lo = (wgt - hi.astype(F32)).astype(BF16)
        wide = _dot(jnp.concatenate([hi, lo], axis=-1), expand_ref[...])
        o_g = jnp.concatenate([o_stage[gi * n_cb + cb] for cb in range(n_cb)], axis=-1)
        term = wide * o_g
        yb = term if yb is None else yb + term
    y = jnp.concatenate([ya_ref[0], yb.astype(BF16)], axis=-1)
    x = x_ref[0] + _dot(y, wo_ref[...])

    hb = _rmsnorm(x, ln_ref[...]).astype(BF16)
    for cb in range(D_FF // MXU_COLS):
        cols = slice(cb * MXU_COLS, (cb + 1) * MXU_COLS)
        gate = _causal_conv_cols(ext_ref, cw_ref, _dot(hb, wg_ref[:, cols]), cols, tm,
                                 FFN_CONV_WIDTH)
        act_ref[:, cols] = (gate * _sigmoid(gate) * _dot(hb, wu_ref[:, cols])).astype(BF16)
    out = x + _dot(act_ref[...], wd_ref[...])
    if final_norm:
        out = _rmsnorm(out, lnf_ref[...])
    out_ref[0] = out


def _mix_ffn(x, ya, os_, lses, w_out, ln, wg, wu, cw, wd, lnf, final_norm):
    bn, s, d = x.shape
    tm = min(TOKEN_TILE, s)
    head_of_lane = np.full((LANES,), -1)
    head_of_lane[:SWA_HEADS] = ATTN_LSE_HEAD_ORDER
    expand = jnp.asarray(np.tile(
        head_of_lane[:, None] == (np.arange(SWA_WIDTH)[None, :] // SWA_HEAD_DIM), (2, 1)),
        dtype=BF16)
    tok = lambda w: pl.BlockSpec((1, tm, w), lambda b, j: (b, j, 0))
    view = lambda w: [pl.BlockSpec((1, tm // dil, dil * w), lambda b, j: (b, j, 0))
                      for _, dil in DILATED_CONFIGS]
    const = lambda a: pl.BlockSpec(a.shape, lambda b, j: (0,) * a.ndim,
                                   pipeline_mode=pl.Buffered(1))
    n_cfg = len(DILATED_CONFIGS)
    consts = (expand, w_out, ln, wg, wu, cw, wd, lnf)
    return pl.pallas_call(
        functools.partial(_mix_ffn_kernel, tm=tm, final_norm=final_norm),
        grid=(bn, s // tm),
        in_specs=[tok(d), tok(GDN_WIDTH)] + view(SWA_WIDTH) + view(LANES)
                 + [const(a) for a in consts],
        out_specs=tok(d),
        out_shape=jax.ShapeDtypeStruct((bn, s, d), F32),
        scratch_shapes=[pltpu.VMEM((n_cfg * SWA_WIDTH // LANES, tm, LANES), F32),
                        pltpu.VMEM((n_cfg, tm, LANES), F32),
                        pltpu.VMEM((tm + SUBLANES, D_FF), F32),
                        pltpu.VMEM((tm, D_FF), BF16)],
        compiler_params=pltpu.CompilerParams(
            dimension_semantics=("arbitrary", "arbitrary"),
            vmem_limit_bytes=VMEM_LIMIT_BYTES),
        name="mix_ffn",
    )(x, ya, *os_, *lses, *consts)


def kernel(x, ln1, w_in, conv_qkv, a_log, dt_bias, gdn_norm, w_out, ln2, w_gate, w_up,
           ffn_conv, w_down, ln_f):
    bn, s, d = x.shape
    depth = w_in.shape[0]
    assert d == D_MODEL and s % GDN_TOKEN_TILE == 0 and s % TOKEN_TILE == 0
    for window, dilation in DILATED_CONFIGS:
        assert window // dilation <= BAND_BLOCK and (s // dilation) % BAND_BLOCK == 0

    c_z = 3 * GDN_WIDTH
    c_b = c_z + GDN_WIDTH
    c_a = c_b + GDN_HEADS
    c_q = c_a + GDN_HEADS
    rep = lambda v: jnp.repeat(v, GDN_HEAD_DIM, axis=-1)
    row = lambda v: v.reshape(1, -1)

    for l in range(depth):
        w = w_in[l]
        wa = w[:, :c_b].astype(BF16)
        wg = jnp.concatenate([rep(w[:, c_a:c_q]), rep(w[:, c_b:c_a])], axis=1).astype(BF16)
        wb = w[:, c_q:].astype(BF16)
        qkv, z, g, beta, *qkvb_views = _in_proj(
            x, row(ln1[l]), wa, wg, wb, conv_qkv[l], row(rep(a_log[l])), row(rep(dt_bias[l])))
        ya = _gdn(qkv, z, g, beta, row(gdn_norm[l]))
        os_, lses = zip(*[_attn(view, window, dilation)
                          for view, (window, dilation) in zip(qkvb_views, DILATED_CONFIGS)])
        x = _mix_ffn(x, ya, os_, lses, w_out[l].astype(BF16), row(ln2[l]),
                     w_gate[l].astype(BF16), w_up[l].astype(BF16), ffn_conv[l],
                     w_down[l].astype(BF16), row(ln_f), final_norm=(l == depth - 1))
    return x
```
